```python
import math
import jax, jax.numpy as jnp
from jax import lax
import numpy as np

D_MODEL = 1024
BATCH = 8
SEQ = 2048
DEPTH = 4

EXPAND = 2
E_WIDTH = EXPAND * D_MODEL
HEAD_DIM = 128
SB_HEADS = E_WIDTH // HEAD_DIM
DF_HEADS = E_WIDTH // (2 * HEAD_DIM)
Q_BLOCK = 128
ROPE_THETA = 10000.0
EPS = 1e-6
N_MIXERS = 2
N_SB = (DEPTH + 1) // 2
N_DF = DEPTH // 2

kernel_name = "hybrid_stickbreak_diffattn_gated"


def rms_norm(x, g):
    xf = x.astype(jnp.float32)
    y = xf * lax.rsqrt(jnp.mean(xf * xf, axis=-1, keepdims=True) + EPS)
    return (y * g.astype(jnp.float32)).astype(x.dtype)


def rope_tables(seq, dim):
    inv = 1.0 / (ROPE_THETA ** (jnp.arange(0, dim, 2, dtype=jnp.float32) / dim))
    ang = jnp.arange(seq, dtype=jnp.float32)[:, None] * inv[None, :]
    return jnp.cos(ang), jnp.sin(ang)


def apply_rope(x, cos, sin):
    xf = x.astype(jnp.float32)
    half = xf.shape[-1] // 2
    x1, x2 = xf[..., :half], xf[..., half:]
    out = jnp.concatenate([x1 * cos - x2 * sin, x2 * cos + x1 * sin], axis=-1)
    return out.astype(x.dtype)


def stick_breaking_attention(q, k, v):
    S = q.shape[2]
    scale = 1.0 / math.sqrt(q.shape[-1])
    outs = []
    for t0 in range(0, S, Q_BLOCK):
        t1 = t0 + Q_BLOCK
        z = jnp.einsum('bhqd,bhkd->bhqk', q[:, :, t0:t1], k[:, :, :t1]).astype(jnp.float32) * scale
        t_idx = t0 + jnp.arange(Q_BLOCK)[:, None]
        s_idx = jnp.arange(t1)[None, :]
        mask = s_idx < t_idx
        log_beta = jax.nn.log_sigmoid(z)
        log_1m = jnp.where(mask, jax.nn.log_sigmoid(-z), 0.0)
        suffix = lax.cumsum(log_1m, axis=3, reverse=True) - log_1m
        attn = jnp.where(mask, jnp.exp(log_beta + suffix), 0.0)
        outs.append(jnp.einsum('bhqk,bhkd->bhqd', attn.astype(v.dtype), v[:, :, :t1]))
    return jnp.concatenate(outs, axis=2)


def differential_attention(q, k, v, lam):
    S = q.shape[3]
    scale = 1.0 / math.sqrt(q.shape[-1])
    outs = []
    for t0 in range(0, S, Q_BLOCK):
        t1 = t0 + Q_BLOCK
        z = jnp.einsum('bhmqd,bhmkd->bhmqk', q[:, :, :, t0:t1], k[:, :, :, :t1]).astype(jnp.float32) * scale
        t_idx = t0 + jnp.arange(Q_BLOCK)[:, None]
        s_idx = jnp.arange(t1)[None, :]
        z = jnp.where(s_idx <= t_idx, z, -jnp.inf)
        p = jax.nn.softmax(z, axis=-1)
        a = p[:, :, 0] - lam * p[:, :, 1]
        outs.append(jnp.einsum('bhqk,bhkd->bhqd', a.astype(v.dtype), v[:, :, :t1]))
    return jnp.concatenate(outs, axis=2)


def setup_inputs(seed: int = 0) -> dict:
    key = jax.random.key(seed)
    ks = jax.random.split(key, 16)
    f32 = jnp.float32
    x = jax.random.normal(ks[0], (BATCH, SEQ, D_MODEL), f32)
    sb_norm = 1.0 + 0.02 * jax.random.normal(ks[1], (N_SB, D_MODEL), f32)
    sb_w_in = jax.random.normal(ks[2], (N_SB, D_MODEL, 4 * E_WIDTH), f32) * D_MODEL ** -0.5
    sb_w_out = jax.random.normal(ks[3], (N_SB, E_WIDTH, D_MODEL), f32) * E_WIDTH ** -0.5
    df_norm = 1.0 + 0.02 * jax.random.normal(ks[4], (N_DF, D_MODEL), f32)
    df_w_in = jax.random.normal(ks[5], (N_DF, D_MODEL, 4 * E_WIDTH), f32) * D_MODEL ** -0.5
    df_w_out = jax.random.normal(ks[6], (N_DF, E_WIDTH, D_MODEL), f32) * E_WIDTH ** -0.5
    df_q_norm = 1.0 + 0.02 * jax.random.normal(ks[7], (N_DF, HEAD_DIM), f32)
    df_k_norm = 1.0 + 0.02 * jax.random.normal(ks[8], (N_DF, HEAD_DIM), f32)
    df_lam_q1 = 0.1 * jax.random.normal(ks[9], (N_DF, HEAD_DIM), f32)
    df_lam_k1 = 0.1 * jax.random.normal(ks[10], (N_DF, HEAD_DIM), f32)
    df_lam_q2 = 0.1 * jax.random.normal(ks[11], (N_DF, HEAD_DIM), f32)
    df_lam_k2 = 0.1 * jax.random.normal(ks[12], (N_DF, HEAD_DIM), f32)
    df_sub_norm = 1.0 + 0.02 * jax.random.normal(ks[13], (N_DF, 2 * HEAD_DIM), f32)
    return {"x": x, "sb_norm": sb_norm, "sb_w_in": sb_w_in, "sb_w_out": sb_w_out,
            "df_norm": df_norm, "df_w_in": df_w_in, "df_w_out": df_w_out,
            "df_q_norm": df_q_norm, "df_k_norm": df_k_norm,
            "df_lam_q1": df_lam_q1, "df_lam_k1": df_lam_k1,
            "df_lam_q2": df_lam_q2, "df_lam_k2": df_lam_k2,
            "df_sub_norm": df_sub_norm}


def reference(x, sb_norm, sb_w_in, sb_w_out, df_norm, df_w_in, df_w_out,
              df_q_norm, df_k_norm, df_lam_q1, df_lam_k1, df_lam_q2, df_lam_k2,
              df_sub_norm):
    B, S, _ = x.shape
    cos, sin = rope_tables(S, HEAD_DIM)
    h = x
    for i in range(DEPTH):
        j = i // N_MIXERS
        if i % N_MIXERS == 0:
            u = rms_norm(h, sb_norm[j])
            proj = jnp.einsum('bsd,de->bse', u, sb_w_in[j])
            q, k, v, z = jnp.split(proj, 4, axis=-1)
            to_heads = lambda t: t.reshape(B, S, SB_HEADS, HEAD_DIM).transpose(0, 2, 1, 3)
            o = stick_breaking_attention(to_heads(q), to_heads(k), to_heads(v))
            o = o.transpose(0, 2, 1, 3).reshape(B, S, E_WIDTH)
            y = o * jax.nn.silu(z)
            h = h + jnp.einsum('bse,ed->bsd', y, sb_w_out[j])
        else:
            lam_init = 0.8 - 0.6 * math.exp(-0.3 * i)
            u = rms_norm(h, df_norm[j])
            proj = jnp.einsum('bsd,de->bse', u, df_w_in[j])
            q, k, v, z = jnp.split(proj, 4, axis=-1)
            q = q.reshape(B, S, DF_HEADS, 2, HEAD_DIM).transpose(0, 2, 3, 1, 4)
            k = k.reshape(B, S, DF_HEADS, 2, HEAD_DIM).transpose(0, 2, 3, 1, 4)
            v = v.reshape(B, S, DF_HEADS, 2 * HEAD_DIM).transpose(0, 2, 1, 3)
            q = apply_rope(rms_norm(q, df_q_norm[j]), cos, sin)
            k = apply_rope(rms_norm(k, df_k_norm[j]), cos, sin)
            lam = (jnp.exp(jnp.sum(df_lam_q1[j].astype(jnp.float32) * df_lam_k1[j].astype(jnp.float32)))
                   - jnp.exp(jnp.sum(df_lam_q2[j].astype(jnp.float32) * df_lam_k2[j].astype(jnp.float32)))
                   + lam_init)
            o = differential_attention(q, k, v, lam)
            o = rms_norm(o, df_sub_norm[j]) * (1.0 - lam_init)
            o = o.transpose(0, 2, 1, 3).reshape(B, S, E_WIDTH)
            y = o * jax.nn.silu(z)
            h = h + jnp.einsum('bse,ed->bsd', y, df_w_out[j])
    return h
```

```python
import functools
import math

import jax
import jax.numpy as jnp
from jax import lax
from jax.experimental import pallas as pl
from jax.experimental.pallas import tpu as pltpu

HEAD_DIM = 128
LANES = 128
EPS = 1e-6
ROPE_THETA = 10000.0
N_MIXERS = 2
NEG_BIG = -1e30
VMEM_LIMIT = 48 * 1024 * 1024

F32 = jnp.float32
BF16 = jnp.bfloat16


def _inproj_kernel(x_ref, g_ref, w_ref, cs_ref, cos_ref, sin_ref, o_ref, xn_ref,
                   *, n_rope_tiles):
    j = pl.program_id(1)

    @pl.when(j == 0)
    def _():
        x = x_ref[...]
        ms = jnp.mean(x * x, axis=-1, keepdims=True)
        xn_ref[...] = (x * lax.rsqrt(ms + EPS) * g_ref[...]).astype(BF16)

    acc = jnp.dot(xn_ref[...], w_ref[...], preferred_element_type=F32)
    n_chunks = acc.shape[1] // LANES

    def plain():
        for c in range(n_chunks):
            sl = slice(c * LANES, (c + 1) * LANES)
            o_ref[0, c] = (acc[:, sl] * cs_ref[:, sl]).astype(BF16)

    def qk_norm_rope():
        cos = cos_ref[...]
        sin = sin_ref[...]
        for c in range(n_chunks):
            sl = slice(c * LANES, (c + 1) * LANES)
            xc = acc[:, sl]
            ms = jnp.mean(xc * xc, axis=-1, keepdims=True)
            y = xc * lax.rsqrt(ms + EPS) * cs_ref[:, sl]
            y = y * cos + pltpu.roll(y, HEAD_DIM // 2, 1) * sin
            o_ref[0, c] = y.astype(BF16)

    if n_rope_tiles == 0:
        plain()
    else:
        pl.when(j < n_rope_tiles)(qk_norm_rope)
        pl.when(j >= n_rope_tiles)(plain)


def _inproj(h2d, g, w_bf16, colscale, cos2, sin2, *, batch, seq, n_rope_tiles, tm=1024, tn=1024):
    m, d = h2d.shape
    n = w_bf16.shape[1]
    tm = min(tm, seq)
    spt = seq // tm
    kern = functools.partial(_inproj_kernel, n_rope_tiles=n_rope_tiles)
    return pl.pallas_call(
        kern,
        grid=(m // tm, n // tn),
        in_specs=[
            pl.BlockSpec((tm, d), lambda i, j: (i, 0)),
            pl.BlockSpec((1, d), lambda i, j: (0, 0)),
            pl.BlockSpec((d, tn), lambda i, j: (0, j)),
            pl.BlockSpec((1, tn), lambda i, j: (0, j)),
            pl.BlockSpec((tm, LANES), lambda i, j: (i % spt, 0)),
            pl.BlockSpec((tm, LANES), lambda i, j: (i % spt, 0)),
        ],
        out_specs=pl.BlockSpec((1, tn // LANES, tm, LANES),
                               lambda i, j: (i // spt, j, i % spt, 0)),
        out_shape=jax.ShapeDtypeStruct((batch, n // LANES, seq, LANES), BF16),
        scratch_shapes=[pltpu.VMEM((tm, d), BF16)],
        compiler_params=pltpu.CompilerParams(
            dimension_semantics=("parallel", "arbitrary"),
            vmem_limit_bytes=VMEM_LIMIT),
        name="norm_inproj",
    )(h2d, g, w_bf16, colscale, cos2, sin2)


def _sb_kernel(q_ref, k_ref, v_ref, z_ref, r_ref, o_ref, acc_ref, carry_ref, *, tq, tk):
    qi = pl.program_id(2)
    n_diag = tq // tk
    q = q_ref[0, 0]
    acc_ref[...] = jnp.zeros_like(acc_ref)
    carry_ref[...] = jnp.zeros_like(carry_ref)

    def block(kb, masked):
        ks = pl.multiple_of(kb * tk, tk)
        kblk = k_ref[0, 0, pl.ds(ks, tk), :]
        vblk = v_ref[0, 0, pl.ds(ks, tk), :]
        z = lax.dot_general(q, kblk, (((1,), (1,)), ((), ())),
                            preferred_element_type=F32)
        sp = jnp.log(1.0 + jnp.exp(-jnp.abs(z)))
        log_beta = jnp.minimum(z, 0.0) - sp
        log_1m = log_beta - z
        if masked:
            t_idx = qi * tq + lax.broadcasted_iota(jnp.int32, (tq, tk), 0)
            s_idx = kb * tk + lax.broadcasted_iota(jnp.int32, (tq, tk), 1)
            mask = s_idx < t_idx
            log_1m = jnp.where(mask, log_1m, 0.0)
        hi = log_1m.astype(BF16)
        lo = (log_1m - hi.astype(F32)).astype(BF16)
        res = jnp.dot(jnp.concatenate([hi, lo], axis=1), r_ref[...],
                      preferred_element_type=F32)
        carry = carry_ref[...]
        attn = jnp.exp(log_beta + res[:, :tk] + carry)
        if masked:
            attn = jnp.where(mask, attn, 0.0)
        acc_ref[...] += jnp.dot(attn.astype(BF16), vblk, preferred_element_type=F32)
        carry_ref[...] = carry + res[:, tk:]

    for d in range(n_diag - 1, -1, -1):
        block(qi * n_diag + d, True)

    def body(it, c):
        block(qi * n_diag - 1 - it, False)
        return c

    lax.fori_loop(0, qi * n_diag, body, 0)

    zg = z_ref[0, 0].astype(F32)
    silu = zg * (1.0 / (1.0 + jnp.exp(-zg)))
    o_ref[0, 0] = (acc_ref[...] * silu).astype(BF16)


def _sb_attention(proj, rmat, *, n_heads, tq=256, tk=128):
    batch, _, seq, _ = proj.shape
    kern = functools.partial(_sb_kernel, tq=tq, tk=tk)
    return pl.pallas_call(
        kern,
        grid=(batch, n_heads, seq // tq),
        in_specs=[
            pl.BlockSpec((1, 1, tq, LANES), lambda b, h, i: (b, h, i, 0)),
            pl.BlockSpec((1, 1, seq, LANES), lambda b, h, i: (b, n_heads + h, 0, 0)),
            pl.BlockSpec((1, 1, seq, LANES), lambda b, h, i: (b, 2 * n_heads + h, 0, 0)),
            pl.BlockSpec((1, 1, tq, LANES), lambda b, h, i: (b, 3 * n_heads + h, i, 0)),
            pl.BlockSpec(rmat.shape, lambda b, h, i: (0, 0)),
        ],
        out_specs=pl.BlockSpec((1, 1, tq, LANES), lambda b, h, i: (b, h, i, 0)),
        out_shape=jax.ShapeDtypeStruct((batch, n_heads, seq, LANES), BF16),
        scratch_shapes=[pltpu.VMEM((tq, LANES), F32), pltpu.VMEM((tq, LANES), F32)],
        compiler_params=pltpu.CompilerParams(
            dimension_semantics=("parallel", "parallel", "arbitrary"),
            vmem_limit_bytes=VMEM_LIMIT),
        name="sb_attention",
    )(proj, proj, proj, proj, rmat)


def _df_kernel(lamv_ref, q_ref, k_ref, v_ref, z_ref, sg_ref, o_ref,
               m_ref, l_ref, acc_ref, *, tq, tk, lam_init):
    qi = pl.program_id(2)
    n_diag = tq // tk
    rep = tk // LANES
    m_ref[...] = jnp.full_like(m_ref, NEG_BIG)
    l_ref[...] = jnp.zeros_like(l_ref)
    acc_ref[...] = jnp.zeros_like(acc_ref)

    def widen(x, n):
        return x if n == 1 else jnp.concatenate([x] * n, axis=1)

    def block(kb, masked):
        ks = pl.multiple_of(kb * tk, tk)
        vblk = jnp.concatenate([v_ref[0, 0, pl.ds(ks, tk), :],
                                v_ref[0, 1, pl.ds(ks, tk), :]], axis=1)
        if masked:
            t_idx = qi * tq + lax.broadcasted_iota(jnp.int32, (tq, tk), 0)
            s_idx = kb * tk + lax.broadcasted_iota(jnp.int32, (tq, tk), 1)
            mask = s_idx <= t_idx
        for mi in range(2):
            s = lax.dot_general(q_ref[0, mi], k_ref[0, mi, pl.ds(ks, tk), :],
                                (((1,), (1,)), ((), ())),
                                preferred_element_type=F32)
            if masked:
                s = jnp.where(mask, s, NEG_BIG)
            m_prev = m_ref[mi]
            m_next = jnp.maximum(m_prev, jnp.max(s, axis=1, keepdims=True))
            p = jnp.exp(s - widen(m_next, rep))
            alpha = jnp.exp(m_prev - m_next)
            l_ref[mi] = alpha * l_ref[mi] + jnp.sum(p, axis=1, keepdims=True)
            m_ref[mi] = m_next
            acc_ref[mi] = acc_ref[mi] * widen(alpha, 2) + jnp.dot(
                p.astype(BF16), vblk, preferred_element_type=F32)

    def body(kb, c):
        block(kb, False)
        return c

    lax.fori_loop(0, qi * n_diag, body, 0)
    for d in range(n_diag):
        block(qi * n_diag + d, True)

    lamv = lamv_ref[...]
    lam = (jnp.exp(jnp.sum(lamv[0:1] * lamv[1:2], axis=-1, keepdims=True))
           - jnp.exp(jnp.sum(lamv[2:3] * lamv[3:4], axis=-1, keepdims=True))
           + lam_init)
    o1 = acc_ref[0] / widen(l_ref[0], 2)
    o2 = acc_ref[1] / widen(l_ref[1], 2)
    o = o1 - lam * o2
    ms = jnp.mean(o * o, axis=-1, keepdims=True)
    o = o * lax.rsqrt(ms + EPS) * sg_ref[...] * (1.0 - lam_init)
    zg = jnp.concatenate([z_ref[0, 0], z_ref[0, 1]], axis=1).astype(F32)
    y = o * (zg * (1.0 / (1.0 + jnp.exp(-zg))))
    o_ref[0, 0] = y[:, :LANES].astype(BF16)
    o_ref[0, 1] = y[:, LANES:].astype(BF16)


def _df_attention(proj, lamv, sub_g, *, n_heads, lam_init, tq=256, tk=256):
    batch, _, seq, _ = proj.shape
    kern = functools.partial(_df_kernel, tq=tq, tk=tk, lam_init=lam_init)
    return pl.pallas_call(
        kern,
        grid=(batch, n_heads, seq // tq),
        in_specs=[
            pl.BlockSpec((4, LANES), lambda b, h, i: (0, 0)),
            pl.BlockSpec((1, 2, tq, LANES), lambda b, h, i: (b, h, i, 0)),
            pl.BlockSpec((1, 2, seq, LANES), lambda b, h, i: (b, n_heads + h, 0, 0)),
            pl.BlockSpec((1, 2, seq, LANES), lambda b, h, i: (b, 2 * n_heads + h, 0, 0)),
            pl.BlockSpec((1, 2, tq, LANES), lambda b, h, i: (b, 3 * n_heads + h, i, 0)),
            pl.BlockSpec((1, 2 * LANES), lambda b, h, i: (0, 0)),
        ],
        out_specs=pl.BlockSpec((1, 2, tq, LANES), lambda b, h, i: (b, h, i, 0)),
        out_shape=jax.ShapeDtypeStruct((batch, 2 * n_heads, seq, LANES), BF16),
        scratch_shapes=[pltpu.VMEM((2, tq, LANES), F32), pltpu.VMEM((2, tq, LANES), F32),
                        pltpu.VMEM((2, tq, 2 * LANES), F32)],
        compiler_params=pltpu.CompilerParams(
            dimension_semantics=("parallel", "parallel", "arbitrary"),
            vmem_limit_bytes=VMEM_LIMIT),
        name="df_attention",
    )(lamv, proj, proj, proj, proj, sub_g)


def _outproj_kernel(y_ref, w_ref, h_ref, o_ref):
    n_chunks = y_ref.shape[1]
    y = jnp.concatenate([y_ref[0, c] for c in range(n_chunks)], axis=1)
    o_ref[...] = h_ref[...] + jnp.dot(y, w_ref[...], preferred_element_type=F32)


def _outproj(y, w_bf16, h2d, *, tm=512):
    batch, n_chunks, seq, _ = y.shape
    e, d = w_bf16.shape
    spt = seq // tm
    return pl.pallas_call(
        _outproj_kernel,
        grid=(batch, spt),
        in_specs=[
            pl.BlockSpec((1, n_chunks, tm, LANES), lambda b, i: (b, 0, i, 0)),
            pl.BlockSpec((e, d), lambda b, i: (0, 0)),
            pl.BlockSpec((tm, d), lambda b, i: (b * spt + i, 0)),
        ],
        out_specs=pl.BlockSpec((tm, d), lambda b, i: (b * spt + i, 0)),
        out_shape=jax.ShapeDtypeStruct(h2d.shape, F32),
        compiler_params=pltpu.CompilerParams(
            dimension_semantics=("parallel", "parallel"),
            vmem_limit_bytes=VMEM_LIMIT),
        name="outproj_residual",
    )(y, w_bf16, h2d)


def _rope_tables(seq):
    inv = 1.0 / (ROPE_THETA ** (jnp.arange(0, HEAD_DIM, 2, dtype=F32) / HEAD_DIM))
    ang = jnp.arange(seq, dtype=F32)[:, None] * inv[None, :]
    cos, sin = jnp.cos(ang), jnp.sin(ang)
    return jnp.concatenate([cos, cos], axis=1), jnp.concatenate([-sin, sin], axis=1)


def _suffix_matrix(tk):
    j = jnp.arange(tk)[:, None]
    s = jnp.arange(tk)[None, :]
    t = (j > s).astype(BF16)
    r = jnp.concatenate([t, jnp.ones((tk, LANES), BF16)], axis=1)
    return jnp.concatenate([r, r], axis=0)


def kernel(x, sb_norm, sb_w_in, sb_w_out, df_norm, df_w_in, df_w_out, df_q_norm, df_k_norm,
           df_lam_q1, df_lam_k1, df_lam_q2, df_lam_k2, df_sub_norm):
    batch, seq, d_model = x.shape
    e_width = sb_w_out.shape[1]
    depth = sb_norm.shape[0] + df_norm.shape[0]
    sb_heads = e_width // HEAD_DIM
    df_heads = e_width // (2 * HEAD_DIM)
    scale = 1.0 / math.sqrt(HEAD_DIM)
    sb_tk = 128

    cos2, sin2 = _rope_tables(seq)
    rmat = _suffix_matrix(sb_tk)
    ones_e = jnp.ones((e_width,), F32)
    sb_colscale = jnp.concatenate([ones_e * scale, ones_e, ones_e, ones_e])[None, :]

    h = x.reshape(batch * seq, d_model)
    for i in range(depth):
        j = i // N_MIXERS
        if i % N_MIXERS == 0:
            proj = _inproj(h, sb_norm[j][None, :], sb_w_in[j].astype(BF16), sb_colscale,
                           cos2, sin2, batch=batch, seq=seq, n_rope_tiles=0)
            y = _sb_attention(proj, rmat, n_heads=sb_heads, tk=sb_tk)
            h = _outproj(y, sb_w_out[j].astype(BF16), h)
        else:
            lam_init = 0.8 - 0.6 * math.exp(-0.3 * i)
            colscale = jnp.concatenate([jnp.tile(df_q_norm[j], 2 * df_heads) * scale,
                                        jnp.tile(df_k_norm[j], 2 * df_heads),
                                        ones_e, ones_e])[None, :]
            proj = _inproj(h, df_norm[j][None, :], df_w_in[j].astype(BF16), colscale,
                           cos2, sin2, batch=batch, seq=seq,
                           n_rope_tiles=2 * e_width // 1024)
            lamv = jnp.stack([df_lam_q1[j], df_lam_k1[j], df_lam_q2[j], df_lam_k2[j]])
            y = _df_attention(proj, lamv, df_sub_norm[j][None, :], n_heads=df_heads,
                              lam_init=lam_init)
            h = _outproj(y, df_w_out[j].astype(BF16), h)
    return h.reshape(batch, seq, d_model)
```

```python
import functools
import math

import jax
import jax.numpy as jnp
from jax import lax
from jax.experimental import pallas as pl
from jax.experimental.pallas import tpu as pltpu

HEAD_DIM = 128
LANES = 128
EPS = 1e-6
ROPE_THETA = 10000.0
N_MIXERS = 2
NEG_BIG = -1e30
VMEM_LIMIT = 48 * 1024 * 1024

F32 = jnp.float32
BF16 = jnp.bfloat16


def _inproj_kernel(x_ref, g_ref, w_ref, cs_ref, cos_ref, sin_ref, o_ref, xn_ref,
                   *, n_rope_tiles):
    j = pl.program_id(1)

    @pl.when(j == 0)
    def _():
        x = x_ref[...]
        ms = jnp.mean(x * x, axis=-1, keepdims=True)
        xn_ref[...] = (x * lax.rsqrt(ms + EPS) * g_ref[...]).astype(BF16)

    acc = jnp.dot(xn_ref[...], w_ref[...], preferred_element_type=F32)
    n_chunks = acc.shape[1] // LANES

    def plain():
        for c in range(n_chunks):
            sl = slice(c * LANES, (c + 1) * LANES)
            o_ref[0, c] = (acc[:, sl] * cs_ref[:, sl]).astype(BF16)

    def qk_norm_rope():
        cos = cos_ref[...]
        sin = sin_ref[...]
        for c in range(n_chunks):
            sl = slice(c * LANES, (c + 1) * LANES)
            xc = acc[:, sl]
            ms = jnp.mean(xc * xc, axis=-1, keepdims=True)
            y = xc * lax.rsqrt(ms + EPS) * cs_ref[:, sl]
            y = y * cos + pltpu.roll(y, HEAD_DIM // 2, 1) * sin
            o_ref[0, c] = y.astype(BF16)

    if n_rope_tiles == 0:
        plain()
    else:
        pl.when(j < n_rope_tiles)(qk_norm_rope)
        pl.when(j >= n_rope_tiles)(plain)


def _inproj(h2d, g, w_bf16, colscale, cos2, sin2, *, batch, seq, n_rope_tiles, tm=1024, tn=1024):
    m, d = h2d.shape
    n = w_bf16.shape[1]
    tm = min(tm, seq)
    spt = seq // tm
    kern = functools.partial(_inproj_kernel, n_rope_tiles=n_rope_tiles)
    return pl.pallas_call(
        kern,
        grid=(m // tm, n // tn),
        in_specs=[
            pl.BlockSpec((tm, d), lambda i, j: (i, 0)),
            pl.BlockSpec((1, d), lambda i, j: (0, 0)),
            pl.BlockSpec((d, tn), lambda i, j: (0, j)),
            pl.BlockSpec((1, tn), lambda i, j: (0, j)),
            pl.BlockSpec((tm, LANES), lambda i, j: (i % spt, 0)),
            pl.BlockSpec((tm, LANES), lambda i, j: (i % spt, 0)),
        ],
        out_specs=pl.BlockSpec((1, tn // LANES, tm, LANES),
                               lambda i, j: (i // spt, j, i % spt, 0)),
        out_shape=jax.ShapeDtypeStruct((batch, n // LANES, seq, LANES), BF16),
        scratch_shapes=[pltpu.VMEM((tm, d), BF16)],
        compiler_params=pltpu.CompilerParams(
            dimension_semantics=("parallel", "arbitrary"),
            vmem_limit_bytes=VMEM_LIMIT),
        name="norm_inproj",
    )(h2d, g, w_bf16, colscale, cos2, sin2)


def _sb_kernel(q_ref, k_ref, v_ref, z_ref, r_ref, o_ref, acc_ref, carry_ref, *, tq):
    qi = pl.program_id(2)
    n_sub = tq // LANES
    acc_ref[...] = jnp.zeros_like(acc_ref)
    carry_ref[...] = jnp.zeros_like(carry_ref)

    def log_terms(q, kt):
        z = lax.dot_general(q, kt, (((1,), (1,)), ((), ())), preferred_element_type=F32)
        sp = jnp.log(1.0 + jnp.exp(-jnp.abs(z)))
        log_beta = jnp.minimum(z, 0.0) - sp
        return log_beta, log_beta - z

    def suffix(log_1m):
        hi = log_1m.astype(BF16)
        lo = (log_1m - hi.astype(F32)).astype(BF16)
        res = jnp.dot(jnp.concatenate([hi, lo], axis=1), r_ref[...],
                      preferred_element_type=F32)
        return res[:, :LANES], res[:, LANES:]

    for d in range(n_sub - 1, -1, -1):
        r0 = d * LANES
        ks = pl.multiple_of(qi * tq + r0, LANES)
        log_beta, log_1m = log_terms(q_ref[0, 0, r0:, :], k_ref[0, 0, pl.ds(ks, LANES), :])
        shape = log_beta.shape
        mask = (lax.broadcasted_iota(jnp.int32, shape, 1)
                < lax.broadcasted_iota(jnp.int32, shape, 0))
        suf, rs = suffix(jnp.where(mask, log_1m, 0.0))
        carry = carry_ref[r0:, :]
        attn = jnp.where(mask, jnp.exp(log_beta + suf + carry), 0.0)
        acc_ref[r0:, :] += jnp.dot(attn.astype(BF16), v_ref[0, 0, pl.ds(ks, LANES), :],
                                   preferred_element_type=F32)
        carry_ref[r0:, :] = carry + rs

    def body(it, c):
        ks = pl.multiple_of((qi - 1 - it) * tq, tq)
        log_beta, log_1m = log_terms(q_ref[0, 0], k_ref[0, 0, pl.ds(ks, tq), :])
        base = carry_ref[...]
        parts = [None] * n_sub
        for s in range(n_sub - 1, -1, -1):
            sl = slice(s * LANES, (s + 1) * LANES)
            suf, rs = suffix(log_1m[:, sl])
            parts[s] = jnp.exp(log_beta[:, sl] + suf + base).astype(BF16)
            base = base + rs
        carry_ref[...] = base
        acc_ref[...] += jnp.dot(jnp.concatenate(parts, axis=1), v_ref[0, 0, pl.ds(ks, tq), :],
                                preferred_element_type=F32)
        return c

    lax.fori_loop(0, qi, body, 0)

    zg = z_ref[0, 0].astype(F32)
    silu = zg * (1.0 / (1.0 + jnp.exp(-zg)))
    o_ref[0, 0] = (acc_ref[...] * silu).astype(BF16)


def _sb_attention(proj, rmat, *, n_heads, tq=512):
    batch, _, seq, _ = proj.shape
    tq = min(tq, seq)
    kern = functools.partial(_sb_kernel, tq=tq)
    return pl.pallas_call(
        kern,
        grid=(batch, n_heads, seq // tq),
        in_specs=[
            pl.BlockSpec((1, 1, tq, LANES), lambda b, h, i: (b, h, i, 0)),
            pl.BlockSpec((1, 1, seq, LANES), lambda b, h, i: (b, n_heads + h, 0, 0)),
            pl.BlockSpec((1, 1, seq, LANES), lambda b, h, i: (b, 2 * n_heads + h, 0, 0)),
            pl.BlockSpec((1, 1, tq, LANES), lambda b, h, i: (b, 3 * n_heads + h, i, 0)),
            pl.BlockSpec(rmat.shape, lambda b, h, i: (0, 0)),
        ],
        out_specs=pl.BlockSpec((1, 1, tq, LANES), lambda b, h, i: (b, h, i, 0)),
        out_shape=jax.ShapeDtypeStruct((batch, n_heads, seq, LANES), BF16),
        scratch_shapes=[pltpu.VMEM((tq, LANES), F32), pltpu.VMEM((tq, LANES), F32)],
        compiler_params=pltpu.CompilerParams(
            dimension_semantics=("parallel", "parallel", "arbitrary"),
            vmem_limit_bytes=VMEM_LIMIT),
        name="sb_attention",
    )(proj, proj, proj, proj, rmat)


def _df_kernel(lamv_ref, q_ref, k_ref, v_ref, z_ref, sg_ref, o_ref,
               m_ref, l_ref, acc_ref, *, tq, lam_init):
    qi = pl.program_id(2)
    m_ref[...] = jnp.full_like(m_ref, NEG_BIG)
    l_ref[...] = jnp.zeros_like(l_ref)
    acc_ref[...] = jnp.zeros_like(acc_ref)

    def widen(x, n):
        return x if n == 1 else jnp.concatenate([x] * n, axis=1)

    def tile(r0, nrows, ks, nk, masked):
        rows = slice(r0, r0 + nrows)
        vblk = jnp.concatenate([v_ref[0, 0, pl.ds(ks, nk), :],
                                v_ref[0, 1, pl.ds(ks, nk), :]], axis=1)
        if masked:
            mask = (lax.broadcasted_iota(jnp.int32, (nrows, nk), 1)
                    <= r0 + lax.broadcasted_iota(jnp.int32, (nrows, nk), 0))
        for mi in range(2):
            s = lax.dot_general(q_ref[0, mi, rows, :], k_ref[0, mi, pl.ds(ks, nk), :],
                                (((1,), (1,)), ((), ())),
                                preferred_element_type=F32)
            if masked:
                s = jnp.where(mask, s, NEG_BIG)
            m_prev = m_ref[mi, rows, :]
            m_next = jnp.maximum(m_prev, jnp.max(s, axis=1, keepdims=True))
            p = jnp.exp(s - widen(m_next, nk // LANES))
            alpha = jnp.exp(m_prev - m_next)
            l_ref[mi, rows, :] = alpha * l_ref[mi, rows, :] + jnp.sum(p, axis=1, keepdims=True)
            m_ref[mi, rows, :] = m_next
            acc_ref[mi, rows, :] = acc_ref[mi, rows, :] * widen(alpha, 2) + jnp.dot(
                p.astype(BF16), vblk, preferred_element_type=F32)

    def body(kt, c):
        tile(0, tq, pl.multiple_of(kt * tq, tq), tq, False)
        return c

    lax.fori_loop(0, qi, body, 0)
    ks0 = pl.multiple_of(qi * tq, tq)
    half = tq // 2
    tile(0, half, ks0, half, True)
    tile(half, half, ks0, tq, True)

    lamv = lamv_ref[...]
    lam = (jnp.exp(jnp.sum(lamv[0:1] * lamv[1:2], axis=-1, keepdims=True))
           - jnp.exp(jnp.sum(lamv[2:3] * lamv[3:4], axis=-1, keepdims=True))
           + lam_init)
    o1 = acc_ref[0] / widen(l_ref[0], 2)
    o2 = acc_ref[1] / widen(l_ref[1], 2)
    o = o1 - lam * o2
    ms = jnp.mean(o * o, axis=-1, keepdims=True)
    o = o * lax.rsqrt(ms + EPS) * sg_ref[...] * (1.0 - lam_init)
    zg = jnp.concatenate([z_ref[0, 0], z_ref[0, 1]], axis=1).astype(F32)
    y = o * (zg * (1.0 / (1.0 + jnp.exp(-zg))))
    o_ref[0, 0] = y[:, :LANES].astype(BF16)
    o_ref[0, 1] = y[:, LANES:].astype(BF16)


def _df_attention(proj, lamv, sub_g, *, n_heads, lam_init, tq=512):
    batch, _, seq, _ = proj.shape
    tq = min(tq, seq)
    kern = functools.partial(_df_kernel, tq=tq, lam_init=lam_init)
    return pl.pallas_call(
        kern,
        grid=(batch, n_heads, seq // tq),
        in_specs=[
            pl.BlockSpec((4, LANES), lambda b, h, i: (0, 0)),
            pl.BlockSpec((1, 2, tq, LANES), lambda b, h, i: (b, h, i, 0)),
            pl.BlockSpec((1, 2, seq, LANES), lambda b, h, i: (b, n_heads + h, 0, 0)),
            pl.BlockSpec((1, 2, seq, LANES), lambda b, h, i: (b, 2 * n_heads + h, 0, 0)),
            pl.BlockSpec((1, 2, tq, LANES), lambda b, h, i: (b, 3 * n_heads + h, i, 0)),
            pl.BlockSpec((1, 2 * LANES), lambda b, h, i: (0, 0)),
        ],
        out_specs=pl.BlockSpec((1, 2, tq, LANES), lambda b, h, i: (b, h, i, 0)),
        out_shape=jax.ShapeDtypeStruct((batch, 2 * n_heads, seq, LANES), BF16),
        scratch_shapes=[pltpu.VMEM((2, tq, LANES), F32), pltpu.VMEM((2, tq, LANES), F32),
                        pltpu.VMEM((2, tq, 2 * LANES), F32)],
        compiler_params=pltpu.CompilerParams(
            dimension_semantics=("parallel", "parallel", "arbitrary"),
            vmem_limit_bytes=VMEM_LIMIT),
        name="df_attention",
    )(lamv, proj, proj, proj, proj, sub_g)


def _outproj_kernel(y_ref, w_ref, h_ref, o_ref):
    n_chunks = y_ref.shape[1]
    y = jnp.concatenate([y_ref[0, c] for c in range(n_chunks)], axis=1)
    o_ref[...] = h_ref[...] + jnp.dot(y, w_ref[...], preferred_element_type=F32)


def _outproj(y, w_bf16, h2d, *, tm=512):
    batch, n_chunks, seq, _ = y.shape
    e, d = w_bf16.shape
    spt = seq // tm
    return pl.pallas_call(
        _outproj_kernel,
        grid=(batch, spt),
        in_specs=[
            pl.BlockSpec((1, n_chunks, tm, LANES), lambda b, i: (b, 0, i, 0)),
            pl.BlockSpec((e, d), lambda b, i: (0, 0)),
            pl.BlockSpec((tm, d), lambda b, i: (b * spt + i, 0)),
        ],
        out_specs=pl.BlockSpec((tm, d), lambda b, i: (b * spt + i, 0)),
        out_shape=jax.ShapeDtypeStruct(h2d.shape, F32),
        compiler_params=pltpu.CompilerParams(
            dimension_semantics=("parallel", "parallel"),
            vmem_limit_bytes=VMEM_LIMIT),
        name="outproj_residual",
    )(y, w_bf16, h2d)


def _rope_tables(seq):
    inv = 1.0 / (ROPE_THETA ** (jnp.arange(0, HEAD_DIM, 2, dtype=F32) / HEAD_DIM))
    ang = jnp.arange(seq, dtype=F32)[:, None] * inv[None, :]
    cos, sin = jnp.cos(ang), jnp.sin(ang)
    return jnp.concatenate([cos, cos], axis=1), jnp.concatenate([-sin, sin], axis=1)


def _suffix_matrix(tk):
    j = jnp.arange(tk)[:, None]
    s = jnp.arange(tk)[None, :]
    t = (j > s).astype(BF16)
    r = jnp.concatenate([t, jnp.ones((tk, LANES), BF16)], axis=1)
    return jnp.concatenate([r, r], axis=0)


def kernel(x, sb_norm, sb_w_in, sb_w_out, df_norm, df_w_in, df_w_out, df_q_norm, df_k_norm,
           df_lam_q1, df_lam_k1, df_lam_q2, df_lam_k2, df_sub_norm):
    batch, seq, d_model = x.shape
    e_width = sb_w_out.shape[1]
    depth = sb_norm.shape[0] + df_norm.shape[0]
    sb_heads = e_width // HEAD_DIM
    df_heads = e_width // (2 * HEAD_DIM)
    scale = 1.0 / math.sqrt(HEAD_DIM)

    cos2, sin2 = _rope_tables(seq)
    rmat = _suffix_matrix(LANES)
    ones_e = jnp.ones((e_width,), F32)
    sb_colscale = jnp.concatenate([ones_e * scale, ones_e, ones_e, ones_e])[None, :]

    h = x.reshape(batch * seq, d_model)
    for i in range(depth):
        j = i // N_MIXERS
        if i % N_MIXERS == 0:
            proj = _inproj(h, sb_norm[j][None, :], sb_w_in[j].astype(BF16), sb_colscale,
                           cos2, sin2, batch=batch, seq=seq, n_rope_tiles=0)
            y = _sb_attention(proj, rmat, n_heads=sb_heads)
            h = _outproj(y, sb_w_out[j].astype(BF16), h)
        else:
            lam_init = 0.8 - 0.6 * math.exp(-0.3 * i)
            colscale = jnp.concatenate([jnp.tile(df_q_norm[j], 2 * df_heads) * scale,
                                        jnp.tile(df_k_norm[j], 2 * df_heads),
                                        ones_e, ones_e])[None, :]
            proj = _inproj(h, df_norm[j][None, :], df_w_in[j].astype(BF16), colscale,
                           cos2, sin2, batch=batch, seq=seq,
                           n_rope_tiles=2 * e_width // 1024)
            lamv = jnp.stack([df_lam_q1[j], df_lam_k1[j], df_lam_q2[j], df_lam_k2[j]])
            y = _df_attention(proj, lamv, df_sub_norm[j][None, :], n_heads=df_heads,
                              lam_init=lam_init)
            h = _outproj(y, df_w_out[j].astype(BF16), h)
    return h.reshape(batch, seq, d_model)
```

```python
import functools
import math

import jax
import jax.numpy as jnp
from jax import lax
from jax.experimental import pallas as pl
from jax.experimental.pallas import tpu as pltpu

HEAD_DIM = 128
LANES = 128
EPS = 1e-6
ROPE_THETA = 10000.0
N_MIXERS = 2
NEG_BIG = -1e30
VMEM_LIMIT = 48 * 1024 * 1024
SUFFIX_BLOCK = 256
LOG2E = 1.4426950408889634
SIGN_BIT = 0x80000000

F32 = jnp.float32
BF16 = jnp.bfloat16


def _inproj_kernel(x_ref, g_ref, w_ref, cs_ref, cos_ref, sin_ref, o_ref, xn_ref,
                   *, n_rope_tiles):
    j = pl.program_id(1)

    @pl.when(j == 0)
    def _():
        x = x_ref[...]
        ms = jnp.mean(x * x, axis=-1, keepdims=True)
        xn_ref[...] = (x * lax.rsqrt(ms + EPS) * g_ref[...]).astype(BF16)

    acc = jnp.dot(xn_ref[...], w_ref[...], preferred_element_type=F32)
    n_chunks = acc.shape[1] // LANES

    def plain():
        for c in range(n_chunks):
            sl = slice(c * LANES, (c + 1) * LANES)
            o_ref[0, c] = (acc[:, sl] * cs_ref[:, sl]).astype(BF16)

    def qk_norm_rope():
        cos = cos_ref[...]
        sin = sin_ref[...]
        for c in range(n_chunks):
            sl = slice(c * LANES, (c + 1) * LANES)
            xc = acc[:, sl]
            ms = jnp.mean(xc * xc, axis=-1, keepdims=True)
            y = xc * lax.rsqrt(ms + EPS) * cs_ref[:, sl]
            y = y * cos + pltpu.roll(y, HEAD_DIM // 2, 1) * sin
            o_ref[0, c] = y.astype(BF16)

    if n_rope_tiles == 0:
        plain()
    else:
        pl.when(j < n_rope_tiles)(qk_norm_rope)
        pl.when(j >= n_rope_tiles)(plain)


def _inproj(h2d, g, w_bf16, colscale, cos2, sin2, *, batch, seq, n_rope_tiles, tm=1024, tn=1024):
    m, d = h2d.shape
    n = w_bf16.shape[1]
    tm = min(tm, seq)
    spt = seq // tm
    kern = functools.partial(_inproj_kernel, n_rope_tiles=n_rope_tiles)
    return pl.pallas_call(
        kern,
        grid=(m // tm, n // tn),
        in_specs=[
            pl.BlockSpec((tm, d), lambda i, j: (i, 0)),
            pl.BlockSpec((1, d), lambda i, j: (0, 0)),
            pl.BlockSpec((d, tn), lambda i, j: (0, j)),
            pl.BlockSpec((1, tn), lambda i, j: (0, j)),
            pl.BlockSpec((tm, LANES), lambda i, j: (i % spt, 0)),
            pl.BlockSpec((tm, LANES), lambda i, j: (i % spt, 0)),
        ],
        out_specs=pl.BlockSpec((1, tn // LANES, tm, LANES),
                               lambda i, j: (i // spt, j, i % spt, 0)),
        out_shape=jax.ShapeDtypeStruct((batch, n // LANES, seq, LANES), BF16),
        scratch_shapes=[pltpu.VMEM((tm, d), BF16)],
        compiler_params=pltpu.CompilerParams(
            dimension_semantics=("parallel", "arbitrary"),
            vmem_limit_bytes=VMEM_LIMIT),
        name="norm_inproj",
    )(h2d, g, w_bf16, colscale, cos2, sin2)


def _sb_kernel(q_ref, k_ref, v_ref, z_ref, r_ref, o_ref, acc_ref, carry_ref, *, tq, hpg):
    qi = pl.program_id(2)
    n_sub = tq // LANES
    n_wide = tq // SUFFIX_BLOCK
    acc_ref[...] = jnp.zeros_like(acc_ref)
    carry_ref[...] = jnp.zeros_like(carry_ref)

    def log_terms(q, kt):
        z = lax.dot_general(q, kt, (((1,), (1,)), ((), ())), preferred_element_type=F32)
        neg_abs = pltpu.bitcast(pltpu.bitcast(z, jnp.uint32) | jnp.uint32(SIGN_BIT), F32)
        sp = jnp.log2(1.0 + jnp.exp2(neg_abs))
        log_beta = jnp.minimum(z, 0.0) - sp
        return log_beta, log_beta - z

    def suffix(log_1m):
        n = log_1m.shape[1]
        return jnp.dot(log_1m.astype(BF16), r_ref[:n, :n], preferred_element_type=F32)

    for d in range(n_sub - 1, -1, -1):
        r0 = d * LANES
        ks = pl.multiple_of(qi * tq + r0, LANES)
        for g in range(hpg):
            log_beta, log_1m = log_terms(q_ref[0, g, r0:, :], k_ref[0, g, pl.ds(ks, LANES), :])
            shape = log_beta.shape
            mask = (lax.broadcasted_iota(jnp.int32, shape, 1)
                    < lax.broadcasted_iota(jnp.int32, shape, 0))
            log_1m = jnp.where(mask, log_1m, 0.0)
            carry = carry_ref[g, r0:, :]
            attn = jnp.where(mask, jnp.exp2(log_beta + suffix(log_1m) + carry), 0.0)
            acc_ref[g, r0:, :] += jnp.dot(attn.astype(BF16), v_ref[0, g, pl.ds(ks, LANES), :],
                                          preferred_element_type=F32)
            carry_ref[g, r0:, :] = carry + jnp.sum(log_1m, axis=1, keepdims=True)

    def body(it, c):
        ks = pl.multiple_of((qi - 1 - it) * tq, tq)
        for g in range(hpg):
            log_beta, log_1m = log_terms(q_ref[0, g], k_ref[0, g, pl.ds(ks, tq), :])
            base = carry_ref[g]
            parts = [None] * n_wide
            for s in range(n_wide - 1, -1, -1):
                sl = slice(s * SUFFIX_BLOCK, (s + 1) * SUFFIX_BLOCK)
                tot = (log_beta[:, sl] + suffix(log_1m[:, sl])
                       + jnp.concatenate([base] * (SUFFIX_BLOCK // LANES), axis=1))
                parts[s] = jnp.exp2(tot).astype(BF16)
                base = base + jnp.sum(log_1m[:, sl], axis=1, keepdims=True)
            carry_ref[g] = base
            acc_ref[g] += jnp.dot(jnp.concatenate(parts, axis=1), v_ref[0, g, pl.ds(ks, tq), :],
                                  preferred_element_type=F32)
        return c

    lax.fori_loop(0, qi, body, 0)

    for g in range(hpg):
        zg = z_ref[0, g].astype(F32)
        silu = zg * (1.0 / (1.0 + jnp.exp(-zg)))
        o_ref[0, g] = (acc_ref[g] * silu).astype(BF16)


def _sb_attention(proj, rmat, *, n_heads, tq=512, hpg=4):
    batch, _, seq, _ = proj.shape
    tq = min(tq, seq)
    n_groups = n_heads // hpg
    kern = functools.partial(_sb_kernel, tq=tq, hpg=hpg)
    return pl.pallas_call(
        kern,
        grid=(batch, n_groups, seq // tq),
        in_specs=[
            pl.BlockSpec((1, hpg, tq, LANES), lambda b, h, i: (b, h, i, 0)),
            pl.BlockSpec((1, hpg, seq, LANES), lambda b, h, i: (b, n_groups + h, 0, 0)),
            pl.BlockSpec((1, hpg, seq, LANES), lambda b, h, i: (b, 2 * n_groups + h, 0, 0)),
            pl.BlockSpec((1, hpg, tq, LANES), lambda b, h, i: (b, 3 * n_groups + h, i, 0)),
            pl.BlockSpec(rmat.shape, lambda b, h, i: (0, 0)),
        ],
        out_specs=pl.BlockSpec((1, hpg, tq, LANES), lambda b, h, i: (b, h, i, 0)),
        out_shape=jax.ShapeDtypeStruct((batch, n_heads, seq, LANES), BF16),
        scratch_shapes=[pltpu.VMEM((hpg, tq, LANES), F32), pltpu.VMEM((hpg, tq, LANES), F32)],
        compiler_params=pltpu.CompilerParams(
            dimension_semantics=("parallel", "parallel", "arbitrary"),
            vmem_limit_bytes=VMEM_LIMIT),
        name="sb_attention",
    )(proj, proj, proj, proj, rmat)


def _df_kernel(lamv_ref, q_ref, k_ref, v_ref, z_ref, sg_ref, o_ref,
               m_ref, l_ref, acc_ref, *, tq, hpg, lam_init):
    qi = pl.program_id(2)
    m_ref[...] = jnp.full_like(m_ref, NEG_BIG)
    l_ref[...] = jnp.zeros_like(l_ref)
    acc_ref[...] = jnp.zeros_like(acc_ref)

    def widen(x, n):
        return x if n == 1 else jnp.concatenate([x] * n, axis=1)

    def tile(r0, nrows, ks, nk, masked):
        rows = slice(r0, r0 + nrows)
        if masked:
            mask = (lax.broadcasted_iota(jnp.int32, (nrows, nk), 1)
                    <= r0 + lax.broadcasted_iota(jnp.int32, (nrows, nk), 0))
        for c in range(2 * hpg):
            hd = c // 2
            vblk = jnp.concatenate([v_ref[0, 2 * hd, pl.ds(ks, nk), :],
                                    v_ref[0, 2 * hd + 1, pl.ds(ks, nk), :]], axis=1)
            s = lax.dot_general(q_ref[0, c, rows, :], k_ref[0, c, pl.ds(ks, nk), :],
                                (((1,), (1,)), ((), ())),
                                preferred_element_type=F32)
            if masked:
                s = jnp.where(mask, s, NEG_BIG)
            m_prev = m_ref[c, rows, :]
            m_next = jnp.maximum(m_prev, jnp.max(s, axis=1, keepdims=True))
            p = jnp.exp2(s - widen(m_next, nk // LANES))
            alpha = jnp.exp2(m_prev - m_next)
            l_ref[c, rows, :] = alpha * l_ref[c, rows, :] + jnp.sum(p, axis=1, keepdims=True)
            m_ref[c, rows, :] = m_next
            acc_ref[c, rows, :] = acc_ref[c, rows, :] * widen(alpha, 2) + jnp.dot(
                p.astype(BF16), vblk, preferred_element_type=F32)

    def body(kt, c):
        tile(0, tq, pl.multiple_of(kt * tq, tq), tq, False)
        return c

    lax.fori_loop(0, qi, body, 0)
    ks0 = pl.multiple_of(qi * tq, tq)
    half = tq // 2
    tile(0, half, ks0, half, True)
    tile(half, half, ks0, tq, True)

    lamv = lamv_ref[...]
    lam = (jnp.exp(jnp.sum(lamv[0:1] * lamv[1:2], axis=-1, keepdims=True))
           - jnp.exp(jnp.sum(lamv[2:3] * lamv[3:4], axis=-1, keepdims=True))
           + lam_init)
    for hd in range(hpg):
        o1 = acc_ref[2 * hd] / widen(l_ref[2 * hd], 2)
        o2 = acc_ref[2 * hd + 1] / widen(l_ref[2 * hd + 1], 2)
        o = o1 - lam * o2
        ms = jnp.mean(o * o, axis=-1, keepdims=True)
        o = o * lax.rsqrt(ms + EPS) * sg_ref[...] * (1.0 - lam_init)
        zg = jnp.concatenate([z_ref[0, 2 * hd], z_ref[0, 2 * hd + 1]], axis=1).astype(F32)
        y = o * (zg * (1.0 / (1.0 + jnp.exp(-zg))))
        o_ref[0, 2 * hd] = y[:, :LANES].astype(BF16)
        o_ref[0, 2 * hd + 1] = y[:, LANES:].astype(BF16)


def _df_attention(proj, lamv, sub_g, *, n_heads, lam_init, tq=512, hpg=4):
    batch, _, seq, _ = proj.shape
    tq = min(tq, seq)
    n_groups = n_heads // hpg
    nc = 2 * hpg
    kern = functools.partial(_df_kernel, tq=tq, hpg=hpg, lam_init=lam_init)
    return pl.pallas_call(
        kern,
        grid=(batch, n_groups, seq // tq),
        in_specs=[
            pl.BlockSpec((4, LANES), lambda b, h, i: (0, 0)),
            pl.BlockSpec((1, nc, tq, LANES), lambda b, h, i: (b, h, i, 0)),
            pl.BlockSpec((1, nc, seq, LANES), lambda b, h, i: (b, n_groups + h, 0, 0)),
            pl.BlockSpec((1, nc, seq, LANES), lambda b, h, i: (b, 2 * n_groups + h, 0, 0)),
            pl.BlockSpec((1, nc, tq, LANES), lambda b, h, i: (b, 3 * n_groups + h, i, 0)),
            pl.BlockSpec((1, 2 * LANES), lambda b, h, i: (0, 0)),
        ],
        out_specs=pl.BlockSpec((1, nc, tq, LANES), lambda b, h, i: (b, h, i, 0)),
        out_shape=jax.ShapeDtypeStruct((batch, 2 * n_heads, seq, LANES), BF16),
        scratch_shapes=[pltpu.VMEM((nc, tq, LANES), F32), pltpu.VMEM((nc, tq, LANES), F32),
                        pltpu.VMEM((nc, tq, 2 * LANES), F32)],
        compiler_params=pltpu.CompilerParams(
            dimension_semantics=("parallel", "parallel", "arbitrary"),
            vmem_limit_bytes=VMEM_LIMIT),
        name="df_attention",
    )(lamv, proj, proj, proj, proj, sub_g)


def _outproj_kernel(y_ref, w_ref, h_ref, o_ref):
    n_chunks = y_ref.shape[1]
    y = jnp.concatenate([y_ref[0, c] for c in range(n_chunks)], axis=1)
    o_ref[...] = h_ref[...] + jnp.dot(y, w_ref[...], preferred_element_type=F32)


def _outproj(y, w_bf16, h2d, *, tm=512):
    batch, n_chunks, seq, _ = y.shape
    e, d = w_bf16.shape
    spt = seq // tm
    return pl.pallas_call(
        _outproj_kernel,
        grid=(batch, spt),
        in_specs=[
            pl.BlockSpec((1, n_chunks, tm, LANES), lambda b, i: (b, 0, i, 0)),
            pl.BlockSpec((e, d), lambda b, i: (0, 0)),
            pl.BlockSpec((tm, d), lambda b, i: (b * spt + i, 0)),
        ],
        out_specs=pl.BlockSpec((tm, d), lambda b, i: (b * spt + i, 0)),
        out_shape=jax.ShapeDtypeStruct(h2d.shape, F32),
        compiler_params=pltpu.CompilerParams(
            dimension_semantics=("parallel", "parallel"),
            vmem_limit_bytes=VMEM_LIMIT),
        name="outproj_residual",
    )(y, w_bf16, h2d)


def _rope_tables(seq):
    inv = 1.0 / (ROPE_THETA ** (jnp.arange(0, HEAD_DIM, 2, dtype=F32) / HEAD_DIM))
    ang = jnp.arange(seq, dtype=F32)[:, None] * inv[None, :]
    cos, sin = jnp.cos(ang), jnp.sin(ang)
    return jnp.concatenate([cos, cos], axis=1), jnp.concatenate([-sin, sin], axis=1)


def _suffix_matrix(n):
    j = jnp.arange(n)[:, None]
    s = jnp.arange(n)[None, :]
    return (j > s).astype(BF16)


def kernel(x, sb_norm, sb_w_in, sb_w_out, df_norm, df_w_in, df_w_out, df_q_norm, df_k_norm,
           df_lam_q1, df_lam_k1, df_lam_q2, df_lam_k2, df_sub_norm):
    batch, seq, d_model = x.shape
    e_width = sb_w_out.shape[1]
    depth = sb_norm.shape[0] + df_norm.shape[0]
    sb_heads = e_width // HEAD_DIM
    df_heads = e_width // (2 * HEAD_DIM)
    scale = 1.0 / math.sqrt(HEAD_DIM)

    cos2, sin2 = _rope_tables(seq)
    rmat = _suffix_matrix(SUFFIX_BLOCK)
    ones_e = jnp.ones((e_width,), F32)
    sb_colscale = jnp.concatenate([ones_e * (scale * LOG2E), ones_e, ones_e, ones_e])[None, :]

    h = x.reshape(batch * seq, d_model)
    for i in range(depth):
        j = i // N_MIXERS
        if i % N_MIXERS == 0:
            proj = _inproj(h, sb_norm[j][None, :], sb_w_in[j].astype(BF16), sb_colscale,
                           cos2, sin2, batch=batch, seq=seq, n_rope_tiles=0)
            y = _sb_attention(proj, rmat, n_heads=sb_heads)
            h = _outproj(y, sb_w_out[j].astype(BF16), h)
        else:
            lam_init = 0.8 - 0.6 * math.exp(-0.3 * i)
            colscale = jnp.concatenate([jnp.tile(df_q_norm[j], 2 * df_heads) * (scale * LOG2E),
                                        jnp.tile(df_k_norm[j], 2 * df_heads),
                                        ones_e, ones_e])[None, :]
            proj = _inproj(h, df_norm[j][None, :], df_w_in[j].astype(BF16), colscale,
                           cos2, sin2, batch=batch, seq=seq,
                           n_rope_tiles=2 * e_width // 1024)
            lamv = jnp.stack([df_lam_q1[j], df_lam_k1[j], df_lam_q2[j], df_lam_k2[j]])
            y = _df_attention(proj, lamv, df_sub_norm[j][None, :], n_heads=df_heads,
                              lam_init=lam_init)
            h = _outproj(y, df_w_out[j].astype(BF16), h)
    return h.reshape(batch, seq, d_model)
```

```python
import functools
import math

import jax
import jax.numpy as jnp
from jax import lax
from jax.experimental import pallas as pl
from jax.experimental.pallas import tpu as pltpu

HEAD_DIM = 128
LANES = 128
EPS = 1e-6
ROPE_THETA = 10000.0
N_MIXERS = 2
NEG_BIG = -1e30
VMEM_LIMIT = 48 * 1024 * 1024
MXU_WIDTH = 256
SUFFIX_BLOCK = MXU_WIDTH
LOG2E = 1.4426950408889634

F32 = jnp.float32
BF16 = jnp.bfloat16


def _inproj_kernel(x_ref, g_ref, w_ref, cs_ref, cos_ref, sin_ref, o_ref, xn_ref,
                   *, n_rope_tiles):
    j = pl.program_id(1)

    @pl.when(j == 0)
    def _():
        x = x_ref[...]
        ms = jnp.mean(x * x, axis=-1, keepdims=True)
        xn_ref[...] = (x * lax.rsqrt(ms + EPS) * g_ref[...]).astype(BF16)

    acc = jnp.dot(xn_ref[...], w_ref[0].astype(BF16), preferred_element_type=F32)
    n_chunks = acc.shape[1] // LANES

    def plain():
        for c in range(n_chunks):
            sl = slice(c * LANES, (c + 1) * LANES)
            o_ref[0, c] = (acc[:, sl] * cs_ref[:, sl]).astype(BF16)

    def qk_norm_rope():
        cos = cos_ref[...]
        sin = sin_ref[...]
        for c in range(n_chunks):
            sl = slice(c * LANES, (c + 1) * LANES)
            xc = acc[:, sl]
            ms = jnp.mean(xc * xc, axis=-1, keepdims=True)
            y = xc * lax.rsqrt(ms + EPS) * cs_ref[:, sl]
            y = y * cos + pltpu.roll(y, HEAD_DIM // 2, 1) * sin
            o_ref[0, c] = y.astype(BF16)

    if n_rope_tiles == 0:
        plain()
    else:
        pl.when(j < n_rope_tiles)(qk_norm_rope)
        pl.when(j >= n_rope_tiles)(plain)


def _inproj(h2d, g, w_all, layer, colscale, cos2, sin2, *, batch, seq, n_rope_tiles,
            tm=1024, tn=1024):
    m, d = h2d.shape
    n = w_all.shape[2]
    tm = min(tm, seq)
    spt = seq // tm
    kern = functools.partial(_inproj_kernel, n_rope_tiles=n_rope_tiles)
    return pl.pallas_call(
        kern,
        grid=(m // tm, n // tn),
        in_specs=[
            pl.BlockSpec((tm, d), lambda i, j: (i, 0)),
            pl.BlockSpec((1, d), lambda i, j: (0, 0)),
            pl.BlockSpec((1, d, tn), lambda i, j: (layer, 0, j)),
            pl.BlockSpec((1, tn), lambda i, j: (0, j)),
            pl.BlockSpec((tm, LANES), lambda i, j: (i % spt, 0)),
            pl.BlockSpec((tm, LANES), lambda i, j: (i % spt, 0)),
        ],
        out_specs=pl.BlockSpec((1, tn // LANES, tm, LANES),
                               lambda i, j: (i // spt, j, i % spt, 0)),
        out_shape=jax.ShapeDtypeStruct((batch, n // LANES, seq, LANES), BF16),
        scratch_shapes=[pltpu.VMEM((tm, d), BF16)],
        compiler_params=pltpu.CompilerParams(
            dimension_semantics=("parallel", "arbitrary"),
            vmem_limit_bytes=VMEM_LIMIT),
        name="norm_inproj",
    )(h2d, g, w_all, colscale, cos2, sin2)


def _sb_kernel(q_ref, k_ref, v_ref, z_ref, r_ref, o_ref, acc_ref, carry_ref,
               *, tq, hpg, diag_block):
    qi = pl.program_id(2)
    n_wide = tq // SUFFIX_BLOCK
    acc_ref[...] = jnp.zeros_like(acc_ref)
    carry_ref[...] = jnp.zeros_like(carry_ref)

    def log_terms(q, kt):
        z = lax.dot_general(q, kt, (((1,), (1,)), ((), ())), preferred_element_type=F32)
        neg_part = jnp.minimum(z, 0.0)
        neg_pos = neg_part - z
        sp = jnp.log2(1.0 + jnp.exp2(neg_part + neg_pos))
        return neg_part - sp, neg_pos - sp

    def suffix(log_1m):
        n = log_1m.shape[1]
        return jnp.dot(log_1m.astype(BF16), r_ref[:n, :n], preferred_element_type=F32)

    for d in range(tq // diag_block - 1, -1, -1):
        r0 = d * diag_block
        ks = pl.multiple_of(qi * tq + r0, diag_block)
        for g in range(hpg):
            log_beta, log_1m = log_terms(q_ref[0, g, r0:, :],
                                         k_ref[0, g, pl.ds(ks, diag_block), :])
            shape = log_beta.shape
            mask = (lax.broadcasted_iota(jnp.int32, shape, 1)
                    < lax.broadcasted_iota(jnp.int32, shape, 0))
            log_1m = jnp.where(mask, log_1m, 0.0)
            carry = carry_ref[g, r0:, :]
            tot = (log_beta + suffix(log_1m)
                   + jnp.concatenate([carry] * (diag_block // LANES), axis=1))
            attn = jnp.where(mask, jnp.exp2(tot), 0.0)
            acc_ref[g, r0:, :] += jnp.dot(attn.astype(BF16),
                                          v_ref[0, g, pl.ds(ks, diag_block), :],
                                          preferred_element_type=F32)
            carry_ref[g, r0:, :] = carry + jnp.sum(log_1m, axis=1, keepdims=True)

    def body(it, c):
        ks = pl.multiple_of((qi - 1 - it) * tq, tq)
        for g in range(hpg):
            log_beta, log_1m = log_terms(q_ref[0, g], k_ref[0, g, pl.ds(ks, tq), :])
            base = carry_ref[g]
            parts = [None] * n_wide
            for s in range(n_wide - 1, -1, -1):
                sl = slice(s * SUFFIX_BLOCK, (s + 1) * SUFFIX_BLOCK)
                tot = (log_beta[:, sl] + suffix(log_1m[:, sl])
                       + jnp.concatenate([base] * (SUFFIX_BLOCK // LANES), axis=1))
                parts[s] = jnp.exp2(tot).astype(BF16)
                base = base + jnp.sum(log_1m[:, sl], axis=1, keepdims=True)
            carry_ref[g] = base
            acc_ref[g] += jnp.dot(jnp.concatenate(parts, axis=1), v_ref[0, g, pl.ds(ks, tq), :],
                                  preferred_element_type=F32)
        return c

    lax.fori_loop(0, qi, body, 0)

    for g in range(hpg):
        zg = z_ref[0, g].astype(F32)
        silu = zg * (1.0 / (1.0 + jnp.exp(-zg)))
        o_ref[0, g] = (acc_ref[g] * silu).astype(BF16)


def _sb_attention(proj, rmat, *, n_heads, tq=512, hpg=4, diag_block=SUFFIX_BLOCK):
    batch, _, seq, _ = proj.shape
    tq = min(tq, seq)
    n_groups = n_heads // hpg
    kern = functools.partial(_sb_kernel, tq=tq, hpg=hpg, diag_block=diag_block)
    return pl.pallas_call(
        kern,
        grid=(batch, n_groups, seq // tq),
        in_specs=[
            pl.BlockSpec((1, hpg, tq, LANES), lambda b, h, i: (b, h, i, 0)),
            pl.BlockSpec((1, hpg, seq, LANES), lambda b, h, i: (b, n_groups + h, 0, 0)),
            pl.BlockSpec((1, hpg, seq, LANES), lambda b, h, i: (b, 2 * n_groups + h, 0, 0)),
            pl.BlockSpec((1, hpg, tq, LANES), lambda b, h, i: (b, 3 * n_groups + h, i, 0)),
            pl.BlockSpec(rmat.shape, lambda b, h, i: (0, 0)),
        ],
        out_specs=pl.BlockSpec((1, hpg, tq, LANES), lambda b, h, i: (b, h, i, 0)),
        out_shape=jax.ShapeDtypeStruct((batch, n_heads, seq, LANES), BF16),
        scratch_shapes=[pltpu.VMEM((hpg, tq, LANES), F32), pltpu.VMEM((hpg, tq, LANES), F32)],
        compiler_params=pltpu.CompilerParams(
            dimension_semantics=("parallel", "parallel", "arbitrary"),
            vmem_limit_bytes=VMEM_LIMIT),
        name="sb_attention",
    )(proj, proj, proj, proj, rmat)


def _df_kernel(lamv_ref, q_ref, k_ref, v_ref, z_ref, sg_ref, o_ref,
               m_ref, l_ref, acc_ref, *, tq, hpg, lam_init):
    qi = pl.program_id(2)
    m_ref[...] = jnp.full_like(m_ref, NEG_BIG)
    l_ref[...] = jnp.zeros_like(l_ref)
    acc_ref[...] = jnp.zeros_like(acc_ref)

    def widen(x, n):
        return x if n == 1 else jnp.concatenate([x] * n, axis=1)

    def tile(r0, nrows, ks, nk, masked):
        rows = slice(r0, r0 + nrows)
        if masked:
            mask = (lax.broadcasted_iota(jnp.int32, (nrows, nk), 1)
                    <= r0 + lax.broadcasted_iota(jnp.int32, (nrows, nk), 0))
        for c in range(2 * hpg):
            hd = c // 2
            vblk = jnp.concatenate([v_ref[0, 2 * hd, pl.ds(ks, nk), :],
                                    v_ref[0, 2 * hd + 1, pl.ds(ks, nk), :]], axis=1)
            s = lax.dot_general(q_ref[0, c, rows, :], k_ref[0, c, pl.ds(ks, nk), :],
                                (((1,), (1,)), ((), ())),
                                preferred_element_type=F32)
            if masked:
                s = jnp.where(mask, s, NEG_BIG)
            m_prev = m_ref[c, rows, :]
            m_next = jnp.maximum(m_prev, jnp.max(s, axis=1, keepdims=True))
            p = jnp.exp2(s - widen(m_next, nk // LANES))
            alpha = jnp.exp2(m_prev - m_next)
            l_ref[c, rows, :] = alpha * l_ref[c, rows, :] + jnp.sum(p, axis=1, keepdims=True)
            m_ref[c, rows, :] = m_next
            acc_ref[c, rows, :] = acc_ref[c, rows, :] * widen(alpha, 2) + jnp.dot(
                p.astype(BF16), vblk, preferred_element_type=F32)

    def body(kt, c):
        tile(0, tq, pl.multiple_of(kt * tq, tq), tq, False)
        return c

    lax.fori_loop(0, qi, body, 0)
    ks0 = pl.multiple_of(qi * tq, tq)
    half = tq // 2
    tile(0, half, ks0, half, True)
    tile(half, half, ks0, tq, True)

    lamv = lamv_ref[...]
    lam = (jnp.exp(jnp.sum(lamv[0:1] * lamv[1:2], axis=-1, keepdims=True))
           - jnp.exp(jnp.sum(lamv[2:3] * lamv[3:4], axis=-1, keepdims=True))
           + lam_init)
    for hd in range(hpg):
        o1 = acc_ref[2 * hd] / widen(l_ref[2 * hd], 2)
        o2 = acc_ref[2 * hd + 1] / widen(l_ref[2 * hd + 1], 2)
        o = o1 - lam * o2
        ms = jnp.mean(o * o, axis=-1, keepdims=True)
        o = o * lax.rsqrt(ms + EPS) * sg_ref[...] * (1.0 - lam_init)
        zg = jnp.concatenate([z_ref[0, 2 * hd], z_ref[0, 2 * hd + 1]], axis=1).astype(F32)
        y = o * (zg * (1.0 / (1.0 + jnp.exp(-zg))))
        o_ref[0, 2 * hd] = y[:, :LANES].astype(BF16)
        o_ref[0, 2 * hd + 1] = y[:, LANES:].astype(BF16)


def _df_attention(proj, lamv, sub_g, *, n_heads, lam_init, tq=512, hpg=4):
    batch, _, seq, _ = proj.shape
    tq = min(tq, seq)
    n_groups = n_heads // hpg
    nc = 2 * hpg
    kern = functools.partial(_df_kernel, tq=tq, hpg=hpg, lam_init=lam_init)
    return pl.pallas_call(
        kern,
        grid=(batch, n_groups, seq // tq),
        in_specs=[
            pl.BlockSpec((4, LANES), lambda b, h, i: (0, 0)),
            pl.BlockSpec((1, nc, tq, LANES), lambda b, h, i: (b, h, i, 0)),
            pl.BlockSpec((1, nc, seq, LANES), lambda b, h, i: (b, n_groups + h, 0, 0)),
            pl.BlockSpec((1, nc, seq, LANES), lambda b, h, i: (b, 2 * n_groups + h, 0, 0)),
            pl.BlockSpec((1, nc, tq, LANES), lambda b, h, i: (b, 3 * n_groups + h, i, 0)),
            pl.BlockSpec((1, 2 * LANES), lambda b, h, i: (0, 0)),
        ],
        out_specs=pl.BlockSpec((1, nc, tq, LANES), lambda b, h, i: (b, h, i, 0)),
        out_shape=jax.ShapeDtypeStruct((batch, 2 * n_heads, seq, LANES), BF16),
        scratch_shapes=[pltpu.VMEM((nc, tq, LANES), F32), pltpu.VMEM((nc, tq, LANES), F32),
                        pltpu.VMEM((nc, tq, 2 * LANES), F32)],
        compiler_params=pltpu.CompilerParams(
            dimension_semantics=("parallel", "parallel", "arbitrary"),
            vmem_limit_bytes=VMEM_LIMIT),
        name="df_attention",
    )(lamv, proj, proj, proj, proj, sub_g)


def _outproj_kernel(y_ref, w_ref, h_ref, o_ref, wb_ref):
    @pl.when((pl.program_id(0) == 0) & (pl.program_id(1) == 0))
    def _():
        wb_ref[...] = w_ref[0].astype(BF16)

    n_chunks = y_ref.shape[1]
    y = jnp.concatenate([y_ref[0, c] for c in range(n_chunks)], axis=1)
    o_ref[...] = h_ref[...] + jnp.dot(y, wb_ref[...], preferred_element_type=F32)


def _outproj(y, w_all, layer, h2d, *, tm=512):
    batch, n_chunks, seq, _ = y.shape
    _, e, d = w_all.shape
    tm = min(tm, seq)
    spt = seq // tm
    return pl.pallas_call(
        _outproj_kernel,
        grid=(batch, spt),
        in_specs=[
            pl.BlockSpec((1, n_chunks, tm, LANES), lambda b, i: (b, 0, i, 0)),
            pl.BlockSpec((1, e, d), lambda b, i: (layer, 0, 0)),
            pl.BlockSpec((tm, d), lambda b, i: (b * spt + i, 0)),
        ],
        out_specs=pl.BlockSpec((tm, d), lambda b, i: (b * spt + i, 0)),
        out_shape=jax.ShapeDtypeStruct(h2d.shape, F32),
        scratch_shapes=[pltpu.VMEM((e, d), BF16)],
        compiler_params=pltpu.CompilerParams(
            dimension_semantics=("arbitrary", "arbitrary"),
            vmem_limit_bytes=VMEM_LIMIT),
        name="outproj_residual",
    )(y, w_all, h2d)


def _rope_tables(seq):
    inv = 1.0 / (ROPE_THETA ** (jnp.arange(0, HEAD_DIM, 2, dtype=F32) / HEAD_DIM))
    ang = jnp.arange(seq, dtype=F32)[:, None] * inv[None, :]
    cos, sin = jnp.cos(ang), jnp.sin(ang)
    return jnp.concatenate([cos, cos], axis=1), jnp.concatenate([-sin, sin], axis=1)


def _suffix_matrix(n):
    j = jnp.arange(n)[:, None]
    s = jnp.arange(n)[None, :]
    return (j > s).astype(BF16)


def kernel(x, sb_norm, sb_w_in, sb_w_out, df_norm, df_w_in, df_w_out, df_q_norm, df_k_norm,
           df_lam_q1, df_lam_k1, df_lam_q2, df_lam_k2, df_sub_norm):
    batch, seq, d_model = x.shape
    e_width = sb_w_out.shape[1]
    depth = sb_norm.shape[0] + df_norm.shape[0]
    sb_heads = e_width // HEAD_DIM
    df_heads = e_width // (2 * HEAD_DIM)
    scale = 1.0 / math.sqrt(HEAD_DIM)

    cos2, sin2 = _rope_tables(seq)
    rmat = _suffix_matrix(SUFFIX_BLOCK)
    ones_e = jnp.ones((e_width,), F32)
    sb_colscale = jnp.concatenate([ones_e * (scale * LOG2E), ones_e, ones_e, ones_e])[None, :]

    h = x.reshape(batch * seq, d_model)
    for i in range(depth):
        j = i // N_MIXERS
        if i % N_MIXERS == 0:
            proj = _inproj(h, sb_norm[j][None, :], sb_w_in, j, sb_colscale,
                           cos2, sin2, batch=batch, seq=seq, n_rope_tiles=0)
            y = _sb_attention(proj, rmat, n_heads=sb_heads)
            h = _outproj(y, sb_w_out, j, h)
        else:
            lam_init = 0.8 - 0.6 * math.exp(-0.3 * i)
            colscale = jnp.concatenate([jnp.tile(df_q_norm[j], 2 * df_heads) * (scale * LOG2E),
                                        jnp.tile(df_k_norm[j], 2 * df_heads),
                                        ones_e, ones_e])[None, :]
            proj = _inproj(h, df_norm[j][None, :], df_w_in, j, colscale,
                           cos2, sin2, batch=batch, seq=seq,
                           n_rope_tiles=2 * e_width // 1024)
            lamv = jnp.stack([df_lam_q1[j], df_lam_k1[j], df_lam_q2[j], df_lam_k2[j]])
            y = _df_attention(proj, lamv, df_sub_norm[j][None, :], n_heads=df_heads,
                              lam_init=lam_init)
            h = _outproj(y, df_w_out, j, h)
    return h.reshape(batch, seq, d_model)
```

```python
import functools
import math

import jax
import jax.numpy as jnp
from jax import lax
from jax.experimental import pallas as pl
from jax.experimental.pallas import tpu as pltpu

HEAD_DIM = 128
LANES = 128
EPS = 1e-6
ROPE_THETA = 10000.0
N_MIXERS = 2
NEG_BIG = -1e30
VMEM_LIMIT = 48 * 1024 * 1024
MXU_WIDTH = 256
SUFFIX_BLOCK = MXU_WIDTH
LOG2E = 1.4426950408889634
EXP2_UNDERFLOW = -160.0

F32 = jnp.float32
BF16 = jnp.bfloat16


def _inproj_kernel(x_ref, g_ref, w_ref, cs_ref, cos_ref, sin_ref, o_ref, xn_ref,
                   *, n_rope_tiles):
    j = pl.program_id(1)

    @pl.when(j == 0)
    def _():
        x = x_ref[...]
        ms = jnp.mean(x * x, axis=-1, keepdims=True)
        xn_ref[...] = (x * lax.rsqrt(ms + EPS) * g_ref[...]).astype(BF16)

    acc = jnp.dot(xn_ref[...], w_ref[0].astype(BF16), preferred_element_type=F32)
    n_chunks = acc.shape[1] // LANES

    def plain():
        for c in range(n_chunks):
            sl = slice(c * LANES, (c + 1) * LANES)
            o_ref[0, c] = (acc[:, sl] * cs_ref[:, sl]).astype(BF16)

    def qk_norm_rope():
        cos = cos_ref[...]
        sin = sin_ref[...]
        for c in range(n_chunks):
            sl = slice(c * LANES, (c + 1) * LANES)
            xc = acc[:, sl]
            ms = jnp.mean(xc * xc, axis=-1, keepdims=True)
            y = xc * lax.rsqrt(ms + EPS) * cs_ref[:, sl]
            y = y * cos + pltpu.roll(y, HEAD_DIM // 2, 1) * sin
            o_ref[0, c] = y.astype(BF16)

    if n_rope_tiles == 0:
        plain()
    else:
        pl.when(j < n_rope_tiles)(qk_norm_rope)
        pl.when(j >= n_rope_tiles)(plain)


def _inproj(h2d, g, w_all, layer, colscale, cos2, sin2, *, batch, seq, n_rope_cols,
            tm=2048, tn=512):
    m, d = h2d.shape
    n = w_all.shape[2]
    tm = min(tm, seq)
    spt = seq // tm
    kern = functools.partial(_inproj_kernel, n_rope_tiles=n_rope_cols // tn)
    return pl.pallas_call(
        kern,
        grid=(m // tm, n // tn),
        in_specs=[
            pl.BlockSpec((tm, d), lambda i, j: (i, 0)),
            pl.BlockSpec((1, d), lambda i, j: (0, 0)),
            pl.BlockSpec((1, d, tn), lambda i, j: (layer, 0, j)),
            pl.BlockSpec((1, tn), lambda i, j: (0, j)),
            pl.BlockSpec((tm, LANES), lambda i, j: (i % spt, 0)),
            pl.BlockSpec((tm, LANES), lambda i, j: (i % spt, 0)),
        ],
        out_specs=pl.BlockSpec((1, tn // LANES, tm, LANES),
                               lambda i, j: (i // spt, j, i % spt, 0)),
        out_shape=jax.ShapeDtypeStruct((batch, n // LANES, seq, LANES), BF16),
        scratch_shapes=[pltpu.VMEM((tm, d), BF16)],
        compiler_params=pltpu.CompilerParams(
            dimension_semantics=("parallel", "arbitrary"),
            vmem_limit_bytes=VMEM_LIMIT),
        name="norm_inproj",
    )(h2d, g, w_all, colscale, cos2, sin2)


def _sb_kernel(q_ref, k_ref, v_ref, z_ref, r_ref, o_ref, acc_ref, carry_ref,
               *, tq, hpg, diag_block):
    qi = pl.program_id(2)
    n_wide = tq // SUFFIX_BLOCK
    acc_ref[...] = jnp.zeros_like(acc_ref)
    carry_ref[...] = jnp.zeros_like(carry_ref)

    def log_terms(q, kt):
        z = lax.dot_general(q, kt, (((1,), (1,)), ((), ())), preferred_element_type=F32)
        neg_part = jnp.minimum(z, 0.0)
        neg_pos = neg_part - z
        sp = jnp.log2(1.0 + jnp.exp2(neg_part + neg_pos))
        return neg_part - sp, neg_pos - sp

    def suffix(log_1m):
        n = log_1m.shape[1]
        return jnp.dot(log_1m.astype(BF16), r_ref[:n, :n], preferred_element_type=F32)

    for d in range(tq // diag_block - 1, -1, -1):
        r0 = d * diag_block
        ks = pl.multiple_of(qi * tq + r0, diag_block)
        for g in range(hpg):
            log_beta, log_1m = log_terms(q_ref[0, g, r0:, :],
                                         k_ref[0, g, pl.ds(ks, diag_block), :])
            shape = log_beta.shape
            mask = (lax.broadcasted_iota(jnp.int32, shape, 1)
                    < lax.broadcasted_iota(jnp.int32, shape, 0))
            log_1m = jnp.where(mask, log_1m, 0.0)
            carry = carry_ref[g, r0:, :]
            tot = (log_beta + suffix(log_1m)
                   + jnp.concatenate([carry] * (diag_block // LANES), axis=1))
            attn = jnp.where(mask, jnp.exp2(tot), 0.0)
            acc_ref[g, r0:, :] += jnp.dot(attn.astype(BF16),
                                          v_ref[0, g, pl.ds(ks, diag_block), :],
                                          preferred_element_type=F32)
            carry_ref[g, r0:, :] = carry + jnp.sum(log_1m, axis=1, keepdims=True)

    def still_visible():
        return jnp.max(carry_ref[...]) > EXP2_UNDERFLOW

    def cond(c):
        it, go = c
        return (it < qi) & go

    def body(c):
        it, _ = c
        ks = pl.multiple_of((qi - 1 - it) * tq, tq)
        for g in range(hpg):
            log_beta, log_1m = log_terms(q_ref[0, g], k_ref[0, g, pl.ds(ks, tq), :])
            base = carry_ref[g]
            parts = [None] * n_wide
            for s in range(n_wide - 1, -1, -1):
                sl = slice(s * SUFFIX_BLOCK, (s + 1) * SUFFIX_BLOCK)
                tot = (log_beta[:, sl] + suffix(log_1m[:, sl])
                       + jnp.concatenate([base] * (SUFFIX_BLOCK // LANES), axis=1))
                parts[s] = jnp.exp2(tot).astype(BF16)
                base = base + jnp.sum(log_1m[:, sl], axis=1, keepdims=True)
            carry_ref[g] = base
            acc_ref[g] += jnp.dot(jnp.concatenate(parts, axis=1), v_ref[0, g, pl.ds(ks, tq), :],
                                  preferred_element_type=F32)
        return it + 1, still_visible()

    lax.while_loop(cond, body, (jnp.int32(0), still_visible()))

    for g in range(hpg):
        zg = z_ref[0, g].astype(F32)
        silu = zg * (1.0 / (1.0 + jnp.exp(-zg)))
        o_ref[0, g] = (acc_ref[g] * silu).astype(BF16)


def _sb_attention(proj, rmat, *, n_heads, tq=512, hpg=4, diag_block=SUFFIX_BLOCK):
    batch, _, seq, _ = proj.shape
    tq = min(tq, seq)
    n_groups = n_heads // hpg
    kern = functools.partial(_sb_kernel, tq=tq, hpg=hpg, diag_block=diag_block)
    return pl.pallas_call(
        kern,
        grid=(batch, n_groups, seq // tq),
        in_specs=[
            pl.BlockSpec((1, hpg, tq, LANES), lambda b, h, i: (b, h, i, 0)),
            pl.BlockSpec((1, hpg, seq, LANES), lambda b, h, i: (b, n_groups + h, 0, 0)),
            pl.BlockSpec((1, hpg, seq, LANES), lambda b, h, i: (b, 2 * n_groups + h, 0, 0)),
            pl.BlockSpec((1, hpg, tq, LANES), lambda b, h, i: (b, 3 * n_groups + h, i, 0)),
            pl.BlockSpec(rmat.shape, lambda b, h, i: (0, 0)),
        ],
        out_specs=pl.BlockSpec((1, hpg, tq, LANES), lambda b, h, i: (b, h, i, 0)),
        out_shape=jax.ShapeDtypeStruct((batch, n_heads, seq, LANES), BF16),
        scratch_shapes=[pltpu.VMEM((hpg, tq, LANES), F32), pltpu.VMEM((hpg, tq, LANES), F32)],
        compiler_params=pltpu.CompilerParams(
            dimension_semantics=("parallel", "parallel", "arbitrary"),
            vmem_limit_bytes=VMEM_LIMIT),
        name="sb_attention",
    )(proj, proj, proj, proj, rmat)


def _df_kernel(lamv_ref, q_ref, k_ref, v_ref, z_ref, sg_ref, o_ref,
               m_ref, l_ref, acc_ref, *, tq, hpg, lam_init):
    qi = pl.program_id(2)
    m_ref[...] = jnp.full_like(m_ref, NEG_BIG)
    l_ref[...] = jnp.zeros_like(l_ref)
    acc_ref[...] = jnp.zeros_like(acc_ref)

    def widen(x, n):
        return x if n == 1 else jnp.concatenate([x] * n, axis=1)

    def tile(r0, nrows, ks, nk, masked):
        rows = slice(r0, r0 + nrows)
        if masked:
            mask = (lax.broadcasted_iota(jnp.int32, (nrows, nk), 1)
                    <= r0 + lax.broadcasted_iota(jnp.int32, (nrows, nk), 0))
        for c in range(2 * hpg):
            hd = c // 2
            vblk = jnp.concatenate([v_ref[0, 2 * hd, pl.ds(ks, nk), :],
                                    v_ref[0, 2 * hd + 1, pl.ds(ks, nk), :]], axis=1)
            s = lax.dot_general(q_ref[0, c, rows, :], k_ref[0, c, pl.ds(ks, nk), :],
                                (((1,), (1,)), ((), ())),
                                preferred_element_type=F32)
            if masked:
                s = jnp.where(mask, s, NEG_BIG)
            m_prev = m_ref[c, rows, :]
            m_next = jnp.maximum(m_prev, jnp.max(s, axis=1, keepdims=True))
            p = jnp.exp2(s - widen(m_next, nk // LANES))
            alpha = jnp.exp2(m_prev - m_next)
            l_ref[c, rows, :] = alpha * l_ref[c, rows, :] + jnp.sum(p, axis=1, keepdims=True)
            m_ref[c, rows, :] = m_next
            acc_ref[c, rows, :] = acc_ref[c, rows, :] * widen(alpha, 2) + jnp.dot(
                p.astype(BF16), vblk, preferred_element_type=F32)

    def body(kt, c):
        tile(0, tq, pl.multiple_of(kt * tq, tq), tq, False)
        return c

    lax.fori_loop(0, qi, body, 0)
    ks0 = pl.multiple_of(qi * tq, tq)
    half = tq // 2
    tile(0, half, ks0, half, True)
    tile(half, half, ks0, tq, True)

    lamv = lamv_ref[...]
    lam = (jnp.exp(jnp.sum(lamv[0:1] * lamv[1:2], axis=-1, keepdims=True))
           - jnp.exp(jnp.sum(lamv[2:3] * lamv[3:4], axis=-1, keepdims=True))
           + lam_init)
    for hd in range(hpg):
        o1 = acc_ref[2 * hd] / widen(l_ref[2 * hd], 2)
        o2 = acc_ref[2 * hd + 1] / widen(l_ref[2 * hd + 1], 2)
        o = o1 - lam * o2
        ms = jnp.mean(o * o, axis=-1, keepdims=True)
        o = o * lax.rsqrt(ms + EPS) * sg_ref[...] * (1.0 - lam_init)
        zg = jnp.concatenate([z_ref[0, 2 * hd], z_ref[0, 2 * hd + 1]], axis=1).astype(F32)
        y = o * (zg * (1.0 / (1.0 + jnp.exp(-zg))))
        o_ref[0, 2 * hd] = y[:, :LANES].astype(BF16)
        o_ref[0, 2 * hd + 1] = y[:, LANES:].astype(BF16)


def _df_attention(proj, lamv, sub_g, *, n_heads, lam_init, tq=512, hpg=4):
    batch, _, seq, _ = proj.shape
    tq = min(tq, seq)
    n_groups = n_heads // hpg
    nc = 2 * hpg
    kern = functools.partial(_df_kernel, tq=tq, hpg=hpg, lam_init=lam_init)
    return pl.pallas_call(
        kern,
        grid=(batch, n_groups, seq // tq),
        in_specs=[
            pl.BlockSpec((4, LANES), lambda b, h, i: (0, 0)),
            pl.BlockSpec((1, nc, tq, LANES), lambda b, h, i: (b, h, i, 0)),
            pl.BlockSpec((1, nc, seq, LANES), lambda b, h, i: (b, n_groups + h, 0, 0)),
            pl.BlockSpec((1, nc, seq, LANES), lambda b, h, i: (b, 2 * n_groups + h, 0, 0)),
            pl.BlockSpec((1, nc, tq, LANES), lambda b, h, i: (b, 3 * n_groups + h, i, 0)),
            pl.BlockSpec((1, 2 * LANES), lambda b, h, i: (0, 0)),
        ],
        out_specs=pl.BlockSpec((1, nc, tq, LANES), lambda b, h, i: (b, h, i, 0)),
        out_shape=jax.ShapeDtypeStruct((batch, 2 * n_heads, seq, LANES), BF16),
        scratch_shapes=[pltpu.VMEM((nc, tq, LANES), F32), pltpu.VMEM((nc, tq, LANES), F32),
                        pltpu.VMEM((nc, tq, 2 * LANES), F32)],
        compiler_params=pltpu.CompilerParams(
            dimension_semantics=("parallel", "parallel", "arbitrary"),
            vmem_limit_bytes=VMEM_LIMIT),
        name="df_attention",
    )(lamv, proj, proj, proj, proj, sub_g)


def _outproj_kernel(y_ref, w_ref, h_ref, o_ref, wb_ref):
    @pl.when((pl.program_id(0) == 0) & (pl.program_id(1) == 0))
    def _():
        wb_ref[...] = w_ref[0].astype(BF16)

    n_chunks = y_ref.shape[1]
    y = jnp.concatenate([y_ref[0, c] for c in range(n_chunks)], axis=1)
    o_ref[...] = h_ref[...] + jnp.dot(y, wb_ref[...], preferred_element_type=F32)


def _outproj(y, w_all, layer, h2d, *, tm=512):
    batch, n_chunks, seq, _ = y.shape
    _, e, d = w_all.shape
    tm = min(tm, seq)
    spt = seq // tm
    return pl.pallas_call(
        _outproj_kernel,
        grid=(batch, spt),
        in_specs=[
            pl.BlockSpec((1, n_chunks, tm, LANES), lambda b, i: (b, 0, i, 0)),
            pl.BlockSpec((1, e, d), lambda b, i: (layer, 0, 0)),
            pl.BlockSpec((tm, d), lambda b, i: (b * spt + i, 0)),
        ],
        out_specs=pl.BlockSpec((tm, d), lambda b, i: (b * spt + i, 0)),
        out_shape=jax.ShapeDtypeStruct(h2d.shape, F32),
        scratch_shapes=[pltpu.VMEM((e, d), BF16)],
        compiler_params=pltpu.CompilerParams(
            dimension_semantics=("arbitrary", "arbitrary"),
            vmem_limit_bytes=VMEM_LIMIT),
        name="outproj_residual",
    )(y, w_all, h2d)


def _rope_tables(seq):
    inv = 1.0 / (ROPE_THETA ** (jnp.arange(0, HEAD_DIM, 2, dtype=F32) / HEAD_DIM))
    ang = jnp.arange(seq, dtype=F32)[:, None] * inv[None, :]
    cos, sin = jnp.cos(ang), jnp.sin(ang)
    return jnp.concatenate([cos, cos], axis=1), jnp.concatenate([-sin, sin], axis=1)


def _suffix_matrix(n):
    j = jnp.arange(n)[:, None]
    s = jnp.arange(n)[None, :]
    return (j > s).astype(BF16)


def kernel(x, sb_norm, sb_w_in, sb_w_out, df_norm, df_w_in, df_w_out, df_q_norm, df_k_norm,
           df_lam_q1, df_lam_k1, df_lam_q2, df_lam_k2, df_sub_norm):
    batch, seq, d_model = x.shape
    e_width = sb_w_out.shape[1]
    depth = sb_norm.shape[0] + df_norm.shape[0]
    sb_heads = e_width // HEAD_DIM
    df_heads = e_width // (2 * HEAD_DIM)
    scale = 1.0 / math.sqrt(HEAD_DIM)

    cos2, sin2 = _rope_tables(seq)
    rmat = _suffix_matrix(SUFFIX_BLOCK)
    ones_e = jnp.ones((e_width,), F32)
    sb_colscale = jnp.concatenate([ones_e * (scale * LOG2E), ones_e, ones_e, ones_e])[None, :]

    h = x.reshape(batch * seq, d_model)
    for i in range(depth):
        j = i // N_MIXERS
        if i % N_MIXERS == 0:
            proj = _inproj(h, sb_norm[j][None, :], sb_w_in, j, sb_colscale,
                           cos2, sin2, batch=batch, seq=seq, n_rope_cols=0)
            y = _sb_attention(proj, rmat, n_heads=sb_heads)
            h = _outproj(y, sb_w_out, j, h)
        else:
            lam_init = 0.8 - 0.6 * math.exp(-0.3 * i)
            colscale = jnp.concatenate([jnp.tile(df_q_norm[j], 2 * df_heads) * (scale * LOG2E),
                                        jnp.tile(df_k_norm[j], 2 * df_heads),
                                        ones_e, ones_e])[None, :]
            proj = _inproj(h, df_norm[j][None, :], df_w_in, j, colscale,
                           cos2, sin2, batch=batch, seq=seq,
                           n_rope_cols=2 * e_width)
            lamv = jnp.stack([df_lam_q1[j], df_lam_k1[j], df_lam_q2[j], df_lam_k2[j]])
            y = _df_attention(proj, lamv, df_sub_norm[j][None, :], n_heads=df_heads,
                              lam_init=lam_init)
            h = _outproj(y, df_w_out, j, h)
    return h.reshape(batch, seq, d_model)
```

```python
import functools
import math

import jax
import jax.numpy as jnp
from jax import lax
from jax.experimental import pallas as pl
from jax.experimental.pallas import tpu as pltpu

HEAD_DIM = 128
LANES = 128
EPS = 1e-6
ROPE_THETA = 10000.0
N_MIXERS = 2
NEG_BIG = -1e30
VMEM_LIMIT = 48 * 1024 * 1024
MXU_WIDTH = 256
SUFFIX_BLOCK = MXU_WIDTH
LOG2E = 1.4426950408889634
EXP2_UNDERFLOW = -160.0

F32 = jnp.float32
BF16 = jnp.bfloat16


def _inproj_kernel(x_ref, g_ref, w_ref, cs_ref, cos_ref, sin_ref, hm_ref, o_ref, xn_ref,
                   *, n_rope_tiles):
    j = pl.program_id(1)

    @pl.when(j == 0)
    def _():
        x = x_ref[...]
        ms = jnp.mean(x * x, axis=-1, keepdims=True)
        xn_ref[...] = (x * lax.rsqrt(ms + EPS) * g_ref[...]).astype(BF16)

    acc = jnp.dot(xn_ref[...], w_ref[0].astype(BF16), preferred_element_type=F32)
    n_chunks = acc.shape[1] // LANES

    def plain():
        for c in range(n_chunks):
            sl = slice(c * LANES, (c + 1) * LANES)
            o_ref[0, c] = (acc[:, sl] * cs_ref[:, sl]).astype(BF16)

    def qk_norm_rope():
        cos = cos_ref[...]
        sin = sin_ref[...]
        heads_per_slab = MXU_WIDTH // HEAD_DIM
        for s in range(acc.shape[1] // MXU_WIDTH):
            sl = slice(s * MXU_WIDTH, (s + 1) * MXU_WIDTH)
            xs = acc[:, sl]
            ms = jnp.dot((xs * xs).astype(BF16), hm_ref[...], preferred_element_type=F32)
            y = xs * lax.rsqrt(ms + EPS) * cs_ref[:, sl]
            for c in range(heads_per_slab):
                yc = y[:, c * HEAD_DIM:(c + 1) * HEAD_DIM]
                yc = yc * cos + pltpu.roll(yc, HEAD_DIM // 2, 1) * sin
                o_ref[0, s * heads_per_slab + c] = yc.astype(BF16)

    if n_rope_tiles == 0:
        plain()
    else:
        pl.when(j < n_rope_tiles)(qk_norm_rope)
        pl.when(j >= n_rope_tiles)(plain)


def _head_mean_matrix():
    head = jnp.arange(MXU_WIDTH) // HEAD_DIM
    return jnp.where(head[:, None] == head[None, :], 1.0 / HEAD_DIM, 0.0).astype(BF16)


def _inproj(h2d, g, w_all, layer, colscale, cos2, sin2, *, batch, seq, n_rope_cols,
            tm=2048, tn=512):
    m, d = h2d.shape
    n = w_all.shape[2]
    tm = min(tm, seq)
    spt = seq // tm
    kern = functools.partial(_inproj_kernel, n_rope_tiles=n_rope_cols // tn)
    return pl.pallas_call(
        kern,
        grid=(m // tm, n // tn),
        in_specs=[
            pl.BlockSpec((tm, d), lambda i, j: (i, 0)),
            pl.BlockSpec((1, d), lambda i, j: (0, 0)),
            pl.BlockSpec((1, d, tn), lambda i, j: (layer, 0, j)),
            pl.BlockSpec((1, tn), lambda i, j: (0, j)),
            pl.BlockSpec((tm, LANES), lambda i, j: (i % spt, 0)),
            pl.BlockSpec((tm, LANES), lambda i, j: (i % spt, 0)),
            pl.BlockSpec((MXU_WIDTH, MXU_WIDTH), lambda i, j: (0, 0)),
        ],
        out_specs=pl.BlockSpec((1, tn // LANES, tm, LANES),
                               lambda i, j: (i // spt, j, i % spt, 0)),
        out_shape=jax.ShapeDtypeStruct((batch, n // LANES, seq, LANES), BF16),
        scratch_shapes=[pltpu.VMEM((tm, d), BF16)],
        compiler_params=pltpu.CompilerParams(
            dimension_semantics=("parallel", "arbitrary"),
            vmem_limit_bytes=VMEM_LIMIT),
        name="norm_inproj",
    )(h2d, g, w_all, colscale, cos2, sin2, _head_mean_matrix())


def _sb_kernel(q_ref, k_ref, v_ref, z_ref, r_ref, o_ref, acc_ref, carry_ref,
               *, tq, hpg, diag_block, sweep_rows):
    qi = pl.program_id(2)
    n_wide = tq // SUFFIX_BLOCK
    acc_ref[...] = jnp.zeros_like(acc_ref)
    carry_ref[...] = jnp.zeros_like(carry_ref)

    def log_terms(q, kt):
        z = lax.dot_general(q, kt, (((1,), (1,)), ((), ())), preferred_element_type=F32)
        neg_part = jnp.minimum(z, 0.0)
        neg_pos = neg_part - z
        sp = jnp.log2(1.0 + jnp.exp2(neg_part + neg_pos))
        return neg_part - sp, neg_pos - sp

    def suffix(log_1m):
        n = log_1m.shape[1]
        return jnp.dot(log_1m.astype(BF16), r_ref[:n, :n], preferred_element_type=F32)

    for d in range(tq // diag_block - 1, -1, -1):
        r0 = d * diag_block
        ks = pl.multiple_of(qi * tq + r0, diag_block)
        for g in range(hpg):
            log_beta, log_1m = log_terms(q_ref[0, g, r0:, :],
                                         k_ref[0, g, pl.ds(ks, diag_block), :])
            shape = log_beta.shape
            mask = (lax.broadcasted_iota(jnp.int32, shape, 1)
                    < lax.broadcasted_iota(jnp.int32, shape, 0))
            log_1m = jnp.where(mask, log_1m, 0.0)
            carry = carry_ref[g, r0:, :]
            tot = (log_beta + suffix(log_1m)
                   + jnp.concatenate([carry] * (diag_block // LANES), axis=1))
            attn = jnp.where(mask, jnp.exp2(tot), 0.0)
            acc_ref[g, r0:, :] += jnp.dot(attn.astype(BF16),
                                          v_ref[0, g, pl.ds(ks, diag_block), :],
                                          preferred_element_type=F32)
            carry_ref[g, r0:, :] = carry + jnp.sum(log_1m, axis=1, keepdims=True)

    def sweep(rows):
        def still_visible():
            return jnp.max(carry_ref[:, rows, :]) > EXP2_UNDERFLOW

        def cond(c):
            it, go = c
            return (it < qi) & go

        def body(c):
            it, _ = c
            ks = pl.multiple_of((qi - 1 - it) * tq, tq)
            for g in range(hpg):
                log_beta, log_1m = log_terms(q_ref[0, g, rows, :],
                                             k_ref[0, g, pl.ds(ks, tq), :])
                base = carry_ref[g, rows, :]
                parts = [None] * n_wide
                for s in range(n_wide - 1, -1, -1):
                    sl = slice(s * SUFFIX_BLOCK, (s + 1) * SUFFIX_BLOCK)
                    tot = (log_beta[:, sl] + suffix(log_1m[:, sl])
                           + jnp.concatenate([base] * (SUFFIX_BLOCK // LANES), axis=1))
                    parts[s] = jnp.exp2(tot).astype(BF16)
                    base = base + jnp.sum(log_1m[:, sl], axis=1, keepdims=True)
                carry_ref[g, rows, :] = base
                acc_ref[g, rows, :] += jnp.dot(jnp.concatenate(parts, axis=1),
                                               v_ref[0, g, pl.ds(ks, tq), :],
                                               preferred_element_type=F32)
            return it + 1, still_visible()

        lax.while_loop(cond, body, (jnp.int32(0), still_visible()))

    for r0 in range(0, tq, sweep_rows):
        sweep(slice(r0, r0 + sweep_rows))

    for g in range(hpg):
        zg = z_ref[0, g].astype(F32)
        silu = zg * (1.0 / (1.0 + jnp.exp(-zg)))
        o_ref[0, g] = (acc_ref[g] * silu).astype(BF16)


def _sb_attention(proj, rmat, *, n_heads, tq=512, hpg=4, diag_block=SUFFIX_BLOCK,
                  sweep_rows=256):
    batch, _, seq, _ = proj.shape
    tq = min(tq, seq)
    n_groups = n_heads // hpg
    kern = functools.partial(_sb_kernel, tq=tq, hpg=hpg, diag_block=diag_block,
                             sweep_rows=sweep_rows)
    return pl.pallas_call(
        kern,
        grid=(batch, n_groups, seq // tq),
        in_specs=[
            pl.BlockSpec((1, hpg, tq, LANES), lambda b, h, i: (b, h, i, 0)),
            pl.BlockSpec((1, hpg, seq, LANES), lambda b, h, i: (b, n_groups + h, 0, 0)),
            pl.BlockSpec((1, hpg, seq, LANES), lambda b, h, i: (b, 2 * n_groups + h, 0, 0)),
            pl.BlockSpec((1, hpg, tq, LANES), lambda b, h, i: (b, 3 * n_groups + h, i, 0)),
            pl.BlockSpec(rmat.shape, lambda b, h, i: (0, 0)),
        ],
        out_specs=pl.BlockSpec((1, hpg, tq, LANES), lambda b, h, i: (b, h, i, 0)),
        out_shape=jax.ShapeDtypeStruct((batch, n_heads, seq, LANES), BF16),
        scratch_shapes=[pltpu.VMEM((hpg, tq, LANES), F32), pltpu.VMEM((hpg, tq, LANES), F32)],
        compiler_params=pltpu.CompilerParams(
            dimension_semantics=("parallel", "parallel", "arbitrary"),
            vmem_limit_bytes=VMEM_LIMIT),
        name="sb_attention",
    )(proj, proj, proj, proj, rmat)


def _df_kernel(lamv_ref, q_ref, k_ref, v_ref, z_ref, sg_ref, o_ref,
               m_ref, l_ref, acc_ref, *, tq, hpg, lam_init):
    qi = pl.program_id(2)
    m_ref[...] = jnp.full_like(m_ref, NEG_BIG)
    l_ref[...] = jnp.zeros_like(l_ref)
    acc_ref[...] = jnp.zeros_like(acc_ref)

    def widen(x, n):
        return x if n == 1 else jnp.concatenate([x] * n, axis=1)

    def tile(r0, nrows, ks, nk, masked):
        rows = slice(r0, r0 + nrows)
        if masked:
            mask = (lax.broadcasted_iota(jnp.int32, (nrows, nk), 1)
                    <= r0 + lax.broadcasted_iota(jnp.int32, (nrows, nk), 0))
        for c in range(2 * hpg):
            hd = c // 2
            vblk = jnp.concatenate([v_ref[0, 2 * hd, pl.ds(ks, nk), :],
                                    v_ref[0, 2 * hd + 1, pl.ds(ks, nk), :]], axis=1)
            s = lax.dot_general(q_ref[0, c, rows, :], k_ref[0, c, pl.ds(ks, nk), :],
                                (((1,), (1,)), ((), ())),
                                preferred_element_type=F32)
            if masked:
                s = jnp.where(mask, s, NEG_BIG)
            m_prev = m_ref[c, rows, :]
            m_next = jnp.maximum(m_prev, jnp.max(s, axis=1, keepdims=True))
            p = jnp.exp2(s - widen(m_next, nk // LANES))
            alpha = jnp.exp2(m_prev - m_next)
            l_ref[c, rows, :] = alpha * l_ref[c, rows, :] + jnp.sum(p, axis=1, keepdims=True)
            m_ref[c, rows, :] = m_next
            acc_ref[c, rows, :] = acc_ref[c, rows, :] * widen(alpha, 2) + jnp.dot(
                p.astype(BF16), vblk, preferred_element_type=F32)

    def body(kt, c):
        tile(0, tq, pl.multiple_of(kt * tq, tq), tq, False)
        return c

    lax.fori_loop(0, qi, body, 0)
    ks0 = pl.multiple_of(qi * tq, tq)
    half = tq // 2
    tile(0, half, ks0, half, True)
    tile(half, half, ks0, tq, True)

    lamv = lamv_ref[...]
    lam = (jnp.exp(jnp.sum(lamv[0:1] * lamv[1:2], axis=-1, keepdims=True))
           - jnp.exp(jnp.sum(lamv[2:3] * lamv[3:4], axis=-1, keepdims=True))
           + lam_init)
    for hd in range(hpg):
        o1 = acc_ref[2 * hd] / widen(l_ref[2 * hd], 2)
        o2 = acc_ref[2 * hd + 1] / widen(l_ref[2 * hd + 1], 2)
        o = o1 - lam * o2
        ms = jnp.mean(o * o, axis=-1, keepdims=True)
        o = o * lax.rsqrt(ms + EPS) * sg_ref[...] * (1.0 - lam_init)
        zg = jnp.concatenate([z_ref[0, 2 * hd], z_ref[0, 2 * hd + 1]], axis=1).astype(F32)
        y = o * (zg * (1.0 / (1.0 + jnp.exp(-zg))))
        o_ref[0, 2 * hd] = y[:, :LANES].astype(BF16)
        o_ref[0, 2 * hd + 1] = y[:, LANES:].astype(BF16)


def _df_attention(proj, lamv, sub_g, *, n_heads, lam_init, tq=512, hpg=4):
    batch, _, seq, _ = proj.shape
    tq = min(tq, seq)
    n_groups = n_heads // hpg
    nc = 2 * hpg
    kern = functools.partial(_df_kernel, tq=tq, hpg=hpg, lam_init=lam_init)
    return pl.pallas_call(
        kern,
        grid=(batch, n_groups, seq // tq),
        in_specs=[
            pl.BlockSpec((4, LANES), lambda b, h, i: (0, 0)),
            pl.BlockSpec((1, nc, tq, LANES), lambda b, h, i: (b, h, i, 0)),
            pl.BlockSpec((1, nc, seq, LANES), lambda b, h, i: (b, n_groups + h, 0, 0)),
            pl.BlockSpec((1, nc, seq, LANES), lambda b, h, i: (b, 2 * n_groups + h, 0, 0)),
            pl.BlockSpec((1, nc, tq, LANES), lambda b, h, i: (b, 3 * n_groups + h, i, 0)),
            pl.BlockSpec((1, 2 * LANES), lambda b, h, i: (0, 0)),
        ],
        out_specs=pl.BlockSpec((1, nc, tq, LANES), lambda b, h, i: (b, h, i, 0)),
        out_shape=jax.ShapeDtypeStruct((batch, 2 * n_heads, seq, LANES), BF16),
        scratch_shapes=[pltpu.VMEM((nc, tq, LANES), F32), pltpu.VMEM((nc, tq, LANES), F32),
                        pltpu.VMEM((nc, tq, 2 * LANES), F32)],
        compiler_params=pltpu.CompilerParams(
            dimension_semantics=("parallel", "parallel", "arbitrary"),
            vmem_limit_bytes=VMEM_LIMIT),
        name="df_attention",
    )(lamv, proj, proj, proj, proj, sub_g)


def _outproj_kernel(y_ref, w_ref, h_ref, o_ref, wb_ref):
    @pl.when((pl.program_id(0) == 0) & (pl.program_id(1) == 0))
    def _():
        wb_ref[...] = w_ref[0].astype(BF16)

    n_chunks = y_ref.shape[1]
    y = jnp.concatenate([y_ref[0, c] for c in range(n_chunks)], axis=1)
    o_ref[...] = h_ref[...] + jnp.dot(y, wb_ref[...], preferred_element_type=F32)


def _outproj(y, w_all, layer, h2d, *, tm=512):
    batch, n_chunks, seq, _ = y.shape
    _, e, d = w_all.shape
    tm = min(tm, seq)
    spt = seq // tm
    return pl.pallas_call(
        _outproj_kernel,
        grid=(batch, spt),
        in_specs=[
            pl.BlockSpec((1, n_chunks, tm, LANES), lambda b, i: (b, 0, i, 0)),
            pl.BlockSpec((1, e, d), lambda b, i: (layer, 0, 0)),
            pl.BlockSpec((tm, d), lambda b, i: (b * spt + i, 0)),
        ],
        out_specs=pl.BlockSpec((tm, d), lambda b, i: (b * spt + i, 0)),
        out_shape=jax.ShapeDtypeStruct(h2d.shape, F32),
        scratch_shapes=[pltpu.VMEM((e, d), BF16)],
        compiler_params=pltpu.CompilerParams(
            dimension_semantics=("arbitrary", "arbitrary"),
            vmem_limit_bytes=VMEM_LIMIT),
        name="outproj_residual",
    )(y, w_all, h2d)


def _rope_tables(seq):
    inv = 1.0 / (ROPE_THETA ** (jnp.arange(0, HEAD_DIM, 2, dtype=F32) / HEAD_DIM))
    ang = jnp.arange(seq, dtype=F32)[:, None] * inv[None, :]
    cos, sin = jnp.cos(ang), jnp.sin(ang)
    return jnp.concatenate([cos, cos], axis=1), jnp.concatenate([-sin, sin], axis=1)


def _suffix_matrix(n):
    j = jnp.arange(n)[:, None]
    s = jnp.arange(n)[None, :]
    return (j > s).astype(BF16)


def kernel(x, sb_norm, sb_w_in, sb_w_out, df_norm, df_w_in, df_w_out, df_q_norm, df_k_norm,
           df_lam_q1, df_lam_k1, df_lam_q2, df_lam_k2, df_sub_norm):
    batch, seq, d_model = x.shape
    e_width = sb_w_out.shape[1]
    depth = sb_norm.shape[0] + df_norm.shape[0]
    sb_heads = e_width // HEAD_DIM
    df_heads = e_width // (2 * HEAD_DIM)
    scale = 1.0 / math.sqrt(HEAD_DIM)

    cos2, sin2 = _rope_tables(seq)
    rmat = _suffix_matrix(SUFFIX_BLOCK)
    ones_e = jnp.ones((e_width,), F32)
    sb_colscale = jnp.concatenate([ones_e * (scale * LOG2E), ones_e, ones_e, ones_e])[None, :]

    h = x.reshape(batch * seq, d_model)
    for i in range(depth):
        j = i // N_MIXERS
        if i % N_MIXERS == 0:
            proj = _inproj(h, sb_norm[j][None, :], sb_w_in, j, sb_colscale,
                           cos2, sin2, batch=batch, seq=seq, n_rope_cols=0)
            y = _sb_attention(proj, rmat, n_heads=sb_heads)
            h = _outproj(y, sb_w_out, j, h)
        else:
            lam_init = 0.8 - 0.6 * math.exp(-0.3 * i)
            colscale = jnp.concatenate([jnp.tile(df_q_norm[j], 2 * df_heads) * (scale * LOG2E),
                                        jnp.tile(df_k_norm[j], 2 * df_heads),
                                        ones_e, ones_e])[None, :]
            proj = _inproj(h, df_norm[j][None, :], df_w_in, j, colscale,
                           cos2, sin2, batch=batch, seq=seq,
                           n_rope_cols=2 * e_width)
            lamv = jnp.stack([df_lam_q1[j], df_lam_k1[j], df_lam_q2[j], df_lam_k2[j]])
            y = _df_attention(proj, lamv, df_sub_norm[j][None, :], n_heads=df_heads,
                              lam_init=lam_init)
            h = _outproj(y, df_w_out, j, h)
    return h.reshape(batch, seq, d_model)
```

```python
import functools
import math

import jax
import jax.numpy as jnp
from jax import lax
from jax.experimental import pallas as pl
from jax.experimental.pallas import tpu as pltpu

HEAD_DIM = 128
LANES = 128
EPS = 1e-6
ROPE_THETA = 10000.0
N_MIXERS = 2
NEG_BIG = -1e30
VMEM_LIMIT = 48 * 1024 * 1024
MXU_WIDTH = 256
SUFFIX_BLOCK = MXU_WIDTH
LOG2E = 1.4426950408889634
EXP2_UNDERFLOW = -160.0
SAFE_EXP2_RANGE = 50.0
NORM_SLACK = 1.02

F32 = jnp.float32
BF16 = jnp.bfloat16


def _inproj_kernel(x_ref, g_ref, w_ref, cs_ref, cos_ref, sin_ref, hm_ref, o_ref, xn_ref,
                   *, n_rope_tiles):
    j = pl.program_id(1)

    @pl.when(j == 0)
    def _():
        x = x_ref[...]
        ms = jnp.mean(x * x, axis=-1, keepdims=True)
        xn_ref[...] = (x * lax.rsqrt(ms + EPS) * g_ref[...]).astype(BF16)

    acc = jnp.dot(xn_ref[...], w_ref[0].astype(BF16), preferred_element_type=F32)
    n_chunks = acc.shape[1] // LANES

    def plain():
        for c in range(n_chunks):
            sl = slice(c * LANES, (c + 1) * LANES)
            o_ref[0, c] = (acc[:, sl] * cs_ref[:, sl]).astype(BF16)

    def qk_norm_rope():
        cos = cos_ref[...]
        sin = sin_ref[...]
        heads_per_slab = MXU_WIDTH // HEAD_DIM
        for s in range(acc.shape[1] // MXU_WIDTH):
            sl = slice(s * MXU_WIDTH, (s + 1) * MXU_WIDTH)
            xs = acc[:, sl]
            ms = jnp.dot((xs * xs).astype(BF16), hm_ref[...], preferred_element_type=F32)
            y = xs * lax.rsqrt(ms + EPS) * cs_ref[:, sl]
            for c in range(heads_per_slab):
                yc = y[:, c * HEAD_DIM:(c + 1) * HEAD_DIM]
                yc = yc * cos + pltpu.roll(yc, HEAD_DIM // 2, 1) * sin
                o_ref[0, s * heads_per_slab + c] = yc.astype(BF16)

    if n_rope_tiles == 0:
        plain()
    else:
        pl.when(j < n_rope_tiles)(qk_norm_rope)
        pl.when(j >= n_rope_tiles)(plain)


def _head_mean_matrix():
    head = jnp.arange(MXU_WIDTH) // HEAD_DIM
    return jnp.where(head[:, None] == head[None, :], 1.0 / HEAD_DIM, 0.0).astype(BF16)


def _inproj(h2d, g, w_all, layer, colscale, cos2, sin2, *, batch, seq, n_rope_cols,
            tm=2048, tn=512):
    m, d = h2d.shape
    n = w_all.shape[2]
    tm = min(tm, seq)
    spt = seq // tm
    kern = functools.partial(_inproj_kernel, n_rope_tiles=n_rope_cols // tn)
    return pl.pallas_call(
        kern,
        grid=(m // tm, n // tn),
        in_specs=[
            pl.BlockSpec((tm, d), lambda i, j: (i, 0)),
            pl.BlockSpec((1, d), lambda i, j: (0, 0)),
            pl.BlockSpec((1, d, tn), lambda i, j: (layer, 0, j)),
            pl.BlockSpec((1, tn), lambda i, j: (0, j)),
            pl.BlockSpec((tm, LANES), lambda i, j: (i % spt, 0)),
            pl.BlockSpec((tm, LANES), lambda i, j: (i % spt, 0)),
            pl.BlockSpec((MXU_WIDTH, MXU_WIDTH), lambda i, j: (0, 0)),
        ],
        out_specs=pl.BlockSpec((1, tn // LANES, tm, LANES),
                               lambda i, j: (i // spt, j, i % spt, 0)),
        out_shape=jax.ShapeDtypeStruct((batch, n // LANES, seq, LANES), BF16),
        scratch_shapes=[pltpu.VMEM((tm, d), BF16)],
        compiler_params=pltpu.CompilerParams(
            dimension_semantics=("parallel", "arbitrary"),
            vmem_limit_bytes=VMEM_LIMIT),
        name="norm_inproj",
    )(h2d, g, w_all, colscale, cos2, sin2, _head_mean_matrix())


def _sb_kernel(q_ref, k_ref, v_ref, z_ref, r_ref, o_ref, acc_ref, carry_ref,
               *, tq, hpg, diag_block, sweep_rows):
    qi = pl.program_id(2)
    n_wide = tq // SUFFIX_BLOCK
    acc_ref[...] = jnp.zeros_like(acc_ref)
    carry_ref[...] = jnp.zeros_like(carry_ref)

    def log_terms(q, kt):
        z = lax.dot_general(q, kt, (((1,), (1,)), ((), ())), preferred_element_type=F32)
        neg_part = jnp.minimum(z, 0.0)
        neg_pos = neg_part - z
        sp = jnp.log2(1.0 + jnp.exp2(neg_part + neg_pos))
        return neg_part - sp, neg_pos - sp

    def suffix(log_1m):
        n = log_1m.shape[1]
        return jnp.dot(log_1m.astype(BF16), r_ref[:n, :n], preferred_element_type=F32)

    for d in range(tq // diag_block - 1, -1, -1):
        r0 = d * diag_block
        ks = pl.multiple_of(qi * tq + r0, diag_block)
        for g in range(hpg):
            log_beta, log_1m = log_terms(q_ref[0, g, r0:, :],
                                         k_ref[0, g, pl.ds(ks, diag_block), :])
            shape = log_beta.shape
            mask = (lax.broadcasted_iota(jnp.int32, shape, 1)
                    < lax.broadcasted_iota(jnp.int32, shape, 0))
            log_1m = jnp.where(mask, log_1m, 0.0)
            carry = carry_ref[g, r0:, :]
            tot = (log_beta + suffix(log_1m)
                   + jnp.concatenate([carry] * (diag_block // LANES), axis=1))
            attn = jnp.where(mask, jnp.exp2(tot), 0.0)
            acc_ref[g, r0:, :] += jnp.dot(attn.astype(BF16),
                                          v_ref[0, g, pl.ds(ks, diag_block), :],
                                          preferred_element_type=F32)
            carry_ref[g, r0:, :] = carry + jnp.sum(log_1m, axis=1, keepdims=True)

    def sweep(rows):
        def still_visible():
            return jnp.max(carry_ref[:, rows, :]) > EXP2_UNDERFLOW

        def cond(c):
            it, go = c
            return (it < qi) & go

        def body(c):
            it, _ = c
            ks = pl.multiple_of((qi - 1 - it) * tq, tq)
            for g in range(hpg):
                log_beta, log_1m = log_terms(q_ref[0, g, rows, :],
                                             k_ref[0, g, pl.ds(ks, tq), :])
                base = carry_ref[g, rows, :]
                parts = [None] * n_wide
                for s in range(n_wide - 1, -1, -1):
                    sl = slice(s * SUFFIX_BLOCK, (s + 1) * SUFFIX_BLOCK)
                    tot = (log_beta[:, sl] + suffix(log_1m[:, sl])
                           + jnp.concatenate([base] * (SUFFIX_BLOCK // LANES), axis=1))
                    parts[s] = jnp.exp2(tot).astype(BF16)
                    base = base + jnp.sum(log_1m[:, sl], axis=1, keepdims=True)
                carry_ref[g, rows, :] = base
                acc_ref[g, rows, :] += jnp.dot(jnp.concatenate(parts, axis=1),
                                               v_ref[0, g, pl.ds(ks, tq), :],
                                               preferred_element_type=F32)
            return it + 1, still_visible()

        lax.while_loop(cond, body, (jnp.int32(0), still_visible()))

    for r0 in range(0, tq, sweep_rows):
        sweep(slice(r0, r0 + sweep_rows))

    for g in range(hpg):
        zg = z_ref[0, g].astype(F32)
        silu = zg / (1.0 + jnp.exp(-zg))
        o_ref[0, g] = (acc_ref[g] * silu).astype(BF16)


def _sb_attention(proj, rmat, *, n_heads, tq=512, hpg=4, diag_block=SUFFIX_BLOCK,
                  sweep_rows=256):
    batch, _, seq, _ = proj.shape
    tq = min(tq, seq)
    n_groups = n_heads // hpg
    kern = functools.partial(_sb_kernel, tq=tq, hpg=hpg, diag_block=diag_block,
                             sweep_rows=sweep_rows)
    return pl.pallas_call(
        kern,
        grid=(batch, n_groups, seq // tq),
        in_specs=[
            pl.BlockSpec((1, hpg, tq, LANES), lambda b, h, i: (b, h, i, 0)),
            pl.BlockSpec((1, hpg, seq, LANES), lambda b, h, i: (b, n_groups + h, 0, 0)),
            pl.BlockSpec((1, hpg, seq, LANES), lambda b, h, i: (b, 2 * n_groups + h, 0, 0)),
            pl.BlockSpec((1, hpg, tq, LANES), lambda b, h, i: (b, 3 * n_groups + h, i, 0)),
            pl.BlockSpec(rmat.shape, lambda b, h, i: (0, 0)),
        ],
        out_specs=pl.BlockSpec((1, hpg, tq, LANES), lambda b, h, i: (b, h, i, 0)),
        out_shape=jax.ShapeDtypeStruct((batch, n_heads, seq, LANES), BF16),
        scratch_shapes=[pltpu.VMEM((hpg, tq, LANES), F32), pltpu.VMEM((hpg, tq, LANES), F32)],
        compiler_params=pltpu.CompilerParams(
            dimension_semantics=("parallel", "parallel", "arbitrary"),
            vmem_limit_bytes=VMEM_LIMIT),
        name="sb_attention",
    )(proj, proj, proj, proj, rmat)


def _df_kernel(lamv_ref, gains_ref, q_ref, k_ref, v_ref, z_ref, sg_ref, o_ref,
               m_ref, l_ref, acc_ref, *, tq, hpg, lam_init):
    qi = pl.program_id(2)

    def widen(x, n):
        return x if n == 1 else jnp.concatenate([x] * n, axis=1)

    def scores(c, rows, ks, nk, mask):
        s = lax.dot_general(q_ref[0, c, rows, :], k_ref[0, c, pl.ds(ks, nk), :],
                            (((1,), (1,)), ((), ())), preferred_element_type=F32)
        return s if mask is None else jnp.where(mask, s, NEG_BIG)

    def values(c, ks, nk):
        hd = c // 2
        return jnp.concatenate([v_ref[0, 2 * hd, pl.ds(ks, nk), :],
                                v_ref[0, 2 * hd + 1, pl.ds(ks, nk), :]], axis=1)

    def causal_mask(r0, nrows, nk):
        return (lax.broadcasted_iota(jnp.int32, (nrows, nk), 1)
                <= r0 + lax.broadcasted_iota(jnp.int32, (nrows, nk), 0))

    def tile_online(r0, nrows, ks, nk, masked, first):
        del first
        rows = slice(r0, r0 + nrows)
        mask = causal_mask(r0, nrows, nk) if masked else None
        for c in range(2 * hpg):
            s = scores(c, rows, ks, nk, mask)
            m_prev = m_ref[c, rows, :]
            m_next = jnp.maximum(m_prev, jnp.max(s, axis=1, keepdims=True))
            p = jnp.exp2(s - widen(m_next, nk // LANES))
            alpha = jnp.exp2(m_prev - m_next)
            l_ref[c, rows, :] = alpha * l_ref[c, rows, :] + jnp.sum(p, axis=1, keepdims=True)
            m_ref[c, rows, :] = m_next
            acc_ref[c, rows, :] = acc_ref[c, rows, :] * widen(alpha, 2) + jnp.dot(
                p.astype(BF16), values(c, ks, nk), preferred_element_type=F32)

    def tile_bounded(r0, nrows, ks, nk, masked, first):
        rows = slice(r0, r0 + nrows)
        mask = causal_mask(r0, nrows, nk) if masked else None
        for c in range(2 * hpg):
            p = jnp.exp2(scores(c, rows, ks, nk, mask))
            psum = jnp.sum(p, axis=1, keepdims=True)
            pv = jnp.dot(p.astype(BF16), values(c, ks, nk), preferred_element_type=F32)
            if first:
                l_ref[c, rows, :] = jnp.broadcast_to(psum, (nrows, LANES))
                acc_ref[c, rows, :] = pv
            else:
                l_ref[c, rows, :] += psum
                acc_ref[c, rows, :] += pv

    def sweep(tile):
        ks0 = pl.multiple_of(qi * tq, tq)
        half = tq // 2
        tile(0, half, ks0, half, True, True)
        tile(half, half, ks0, tq, True, True)

        def body(kt, c):
            tile(0, tq, pl.multiple_of(kt * tq, tq), tq, False, False)
            return c

        lax.fori_loop(0, qi, body, 0)

    gains = jnp.abs(gains_ref[...])
    bound = (jnp.max(gains[0:1], axis=-1, keepdims=True)
             * jnp.max(gains[1:2], axis=-1, keepdims=True) * (HEAD_DIM * NORM_SLACK))
    bounded = jnp.max(bound) <= SAFE_EXP2_RANGE

    @pl.when(bounded)
    def _():
        sweep(tile_bounded)

    @pl.when(jnp.logical_not(bounded))
    def _():
        m_ref[...] = jnp.full_like(m_ref, NEG_BIG)
        l_ref[...] = jnp.zeros_like(l_ref)
        acc_ref[...] = jnp.zeros_like(acc_ref)
        sweep(tile_online)

    lamv = lamv_ref[...]
    lam = (jnp.exp(jnp.sum(lamv[0:1] * lamv[1:2], axis=-1, keepdims=True))
           - jnp.exp(jnp.sum(lamv[2:3] * lamv[3:4], axis=-1, keepdims=True))
           + lam_init)
    for hd in range(hpg):
        inv1 = 1.0 / l_ref[2 * hd]
        inv2 = lam / l_ref[2 * hd + 1]
        o = acc_ref[2 * hd] * widen(inv1, 2) - acc_ref[2 * hd + 1] * widen(inv2, 2)
        ms = jnp.mean(o * o, axis=-1, keepdims=True)
        o = o * lax.rsqrt(ms + EPS) * sg_ref[...] * (1.0 - lam_init)
        zg = jnp.concatenate([z_ref[0, 2 * hd], z_ref[0, 2 * hd + 1]], axis=1).astype(F32)
        y = o * (zg / (1.0 + jnp.exp(-zg)))
        o_ref[0, 2 * hd] = y[:, :LANES].astype(BF16)
        o_ref[0, 2 * hd + 1] = y[:, LANES:].astype(BF16)


def _df_attention(proj, lamv, qk_gains, sub_g, *, n_heads, lam_init, tq=512, hpg=4):
    batch, _, seq, _ = proj.shape
    tq = min(tq, seq)
    n_groups = n_heads // hpg
    nc = 2 * hpg
    kern = functools.partial(_df_kernel, tq=tq, hpg=hpg, lam_init=lam_init)
    return pl.pallas_call(
        kern,
        grid=(batch, n_groups, seq // tq),
        in_specs=[
            pl.BlockSpec((4, LANES), lambda b, h, i: (0, 0)),
            pl.BlockSpec((2, LANES), lambda b, h, i: (0, 0)),
            pl.BlockSpec((1, nc, tq, LANES), lambda b, h, i: (b, h, i, 0)),
            pl.BlockSpec((1, nc, seq, LANES), lambda b, h, i: (b, n_groups + h, 0, 0)),
            pl.BlockSpec((1, nc, seq, LANES), lambda b, h, i: (b, 2 * n_groups + h, 0, 0)),
            pl.BlockSpec((1, nc, tq, LANES), lambda b, h, i: (b, 3 * n_groups + h, i, 0)),
            pl.BlockSpec((1, 2 * LANES), lambda b, h, i: (0, 0)),
        ],
        out_specs=pl.BlockSpec((1, nc, tq, LANES), lambda b, h, i: (b, h, i, 0)),
        out_shape=jax.ShapeDtypeStruct((batch, 2 * n_heads, seq, LANES), BF16),
        scratch_shapes=[pltpu.VMEM((nc, tq, LANES), F32), pltpu.VMEM((nc, tq, LANES), F32),
                        pltpu.VMEM((nc, tq, 2 * LANES), F32)],
        compiler_params=pltpu.CompilerParams(
            dimension_semantics=("parallel", "parallel", "arbitrary"),
            vmem_limit_bytes=VMEM_LIMIT),
        name="df_attention",
    )(lamv, qk_gains, proj, proj, proj, proj, sub_g)


def _outproj_kernel(y_ref, w_ref, h_ref, o_ref, wb_ref):
    @pl.when((pl.program_id(0) == 0) & (pl.program_id(1) == 0))
    def _():
        wb_ref[...] = w_ref[0].astype(BF16)

    n_chunks = y_ref.shape[1]
    y = jnp.concatenate([y_ref[0, c] for c in range(n_chunks)], axis=1)
    o_ref[...] = h_ref[...] + jnp.dot(y, wb_ref[...], preferred_element_type=F32)


def _outproj(y, w_all, layer, h2d, *, tm=512):
    batch, n_chunks, seq, _ = y.shape
    _, e, d = w_all.shape
    tm = min(tm, seq)
    spt = seq // tm
    return pl.pallas_call(
        _outproj_kernel,
        grid=(batch, spt),
        in_specs=[
            pl.BlockSpec((1, n_chunks, tm, LANES), lambda b, i: (b, 0, i, 0)),
            pl.BlockSpec((1, e, d), lambda b, i: (layer, 0, 0)),
            pl.BlockSpec((tm, d), lambda b, i: (b * spt + i, 0)),
        ],
        out_specs=pl.BlockSpec((tm, d), lambda b, i: (b * spt + i, 0)),
        out_shape=jax.ShapeDtypeStruct(h2d.shape, F32),
        scratch_shapes=[pltpu.VMEM((e, d), BF16)],
        compiler_params=pltpu.CompilerParams(
            dimension_semantics=("arbitrary", "arbitrary"),
            vmem_limit_bytes=VMEM_LIMIT),
        name="outproj_residual",
    )(y, w_all, h2d)


def _rope_tables(seq):
    inv = 1.0 / (ROPE_THETA ** (jnp.arange(0, HEAD_DIM, 2, dtype=F32) / HEAD_DIM))
    ang = jnp.arange(seq, dtype=F32)[:, None] * inv[None, :]
    cos, sin = jnp.cos(ang), jnp.sin(ang)
    return jnp.concatenate([cos, cos], axis=1), jnp.concatenate([-sin, sin], axis=1)


def _suffix_matrix(n):
    j = jnp.arange(n)[:, None]
    s = jnp.arange(n)[None, :]
    return (j > s).astype(BF16)


def kernel(x, sb_norm, sb_w_in, sb_w_out, df_norm, df_w_in, df_w_out, df_q_norm, df_k_norm,
           df_lam_q1, df_lam_k1, df_lam_q2, df_lam_k2, df_sub_norm):
    batch, seq, d_model = x.shape
    e_width = sb_w_out.shape[1]
    depth = sb_norm.shape[0] + df_norm.shape[0]
    sb_heads = e_width // HEAD_DIM
    df_heads = e_width // (2 * HEAD_DIM)
    scale = 1.0 / math.sqrt(HEAD_DIM)

    cos2, sin2 = _rope_tables(seq)
    rmat = _suffix_matrix(SUFFIX_BLOCK)
    ones_e = jnp.ones((e_width,), F32)
    sb_colscale = jnp.concatenate([ones_e * (scale * LOG2E), ones_e, ones_e, ones_e])[None, :]

    h = x.reshape(batch * seq, d_model)
    for i in range(depth):
        j = i // N_MIXERS
        if i % N_MIXERS == 0:
            proj = _inproj(h, sb_norm[j][None, :], sb_w_in, j, sb_colscale,
                           cos2, sin2, batch=batch, seq=seq, n_rope_cols=0)
            y = _sb_attention(proj, rmat, n_heads=sb_heads)
            h = _outproj(y, sb_w_out, j, h)
        else:
            lam_init = 0.8 - 0.6 * math.exp(-0.3 * i)
            colscale = jnp.concatenate([jnp.tile(df_q_norm[j], 2 * df_heads) * (scale * LOG2E),
                                        jnp.tile(df_k_norm[j], 2 * df_heads),
                                        ones_e, ones_e])[None, :]
            proj = _inproj(h, df_norm[j][None, :], df_w_in, j, colscale,
                           cos2, sin2, batch=batch, seq=seq,
                           n_rope_cols=2 * e_width)
            lamv = jnp.stack([df_lam_q1[j], df_lam_k1[j], df_lam_q2[j], df_lam_k2[j]])
            qk_gains = jnp.stack([df_q_norm[j] * (scale * LOG2E), df_k_norm[j]])
            y = _df_attention(proj, lamv, qk_gains, df_sub_norm[j][None, :], n_heads=df_heads,
                              lam_init=lam_init)
            h = _outproj(y, df_w_out, j, h)
    return h.reshape(batch, seq, d_model)
```

```python
import functools
import math

import jax
import jax.numpy as jnp
from jax import lax
from jax.experimental import pallas as pl
from jax.experimental.pallas import tpu as pltpu

HEAD_DIM = 128
LANES = 128
EPS = 1e-6
ROPE_THETA = 10000.0
N_MIXERS = 2
NEG_BIG = -1e30
VMEM_LIMIT = 48 * 1024 * 1024
MXU_WIDTH = 256
SUFFIX_BLOCK = MXU_WIDTH
LOG2E = 1.4426950408889634
EXP2_UNDERFLOW = -160.0
SAFE_EXP2_RANGE = 50.0
NORM_SLACK = 1.02

F32 = jnp.float32
BF16 = jnp.bfloat16


def _inproj_kernel(x_ref, g_ref, w_ref, cs_ref, cos_ref, sin_ref, hm_ref, o_ref, xn_ref,
                   *, n_rope_tiles):
    j = pl.program_id(1)

    @pl.when(j == 0)
    def _():
        x = x_ref[...]
        ms = jnp.mean(x * x, axis=-1, keepdims=True)
        xn_ref[...] = (x * lax.rsqrt(ms + EPS) * g_ref[...]).astype(BF16)

    acc = jnp.dot(xn_ref[...], w_ref[0].astype(BF16), preferred_element_type=F32)
    n_chunks = acc.shape[1] // LANES

    def plain():
        for c in range(n_chunks):
            sl = slice(c * LANES, (c + 1) * LANES)
            o_ref[0, c] = (acc[:, sl] * cs_ref[:, sl]).astype(BF16)

    def qk_norm_rope():
        cos = cos_ref[...]
        sin = sin_ref[...]
        heads_per_slab = MXU_WIDTH // HEAD_DIM
        for s in range(acc.shape[1] // MXU_WIDTH):
            sl = slice(s * MXU_WIDTH, (s + 1) * MXU_WIDTH)
            xs = acc[:, sl]
            ms = jnp.dot((xs * xs).astype(BF16), hm_ref[...], preferred_element_type=F32)
            y = xs * lax.rsqrt(ms + EPS) * cs_ref[:, sl]
            for c in range(heads_per_slab):
                yc = y[:, c * HEAD_DIM:(c + 1) * HEAD_DIM]
                yc = yc * cos + pltpu.roll(yc, HEAD_DIM // 2, 1) * sin
                o_ref[0, s * heads_per_slab + c] = yc.astype(BF16)

    if n_rope_tiles == 0:
        plain()
    else:
        pl.when(j < n_rope_tiles)(qk_norm_rope)
        pl.when(j >= n_rope_tiles)(plain)


def _head_mean_matrix():
    head = jnp.arange(MXU_WIDTH) // HEAD_DIM
    return jnp.where(head[:, None] == head[None, :], 1.0 / HEAD_DIM, 0.0).astype(BF16)


def _inproj(h2d, g, w_all, layer, colscale, cos2, sin2, *, batch, seq, n_rope_cols,
            tm=2048, tn=512):
    m, d = h2d.shape
    n = w_all.shape[2]
    tm = min(tm, seq)
    spt = seq // tm
    kern = functools.partial(_inproj_kernel, n_rope_tiles=n_rope_cols // tn)
    return pl.pallas_call(
        kern,
        grid=(m // tm, n // tn),
        in_specs=[
            pl.BlockSpec((tm, d), lambda i, j: (i, 0)),
            pl.BlockSpec((1, d), lambda i, j: (0, 0)),
            pl.BlockSpec((1, d, tn), lambda i, j: (layer, 0, j)),
            pl.BlockSpec((1, tn), lambda i, j: (0, j)),
            pl.BlockSpec((tm, LANES), lambda i, j: (i % spt, 0)),
            pl.BlockSpec((tm, LANES), lambda i, j: (i % spt, 0)),
            pl.BlockSpec((MXU_WIDTH, MXU_WIDTH), lambda i, j: (0, 0)),
        ],
        out_specs=pl.BlockSpec((1, tn // LANES, tm, LANES),
                               lambda i, j: (i // spt, j, i % spt, 0)),
        out_shape=jax.ShapeDtypeStruct((batch, n // LANES, seq, LANES), BF16),
        scratch_shapes=[pltpu.VMEM((tm, d), BF16)],
        compiler_params=pltpu.CompilerParams(
            dimension_semantics=("parallel", "arbitrary"),
            vmem_limit_bytes=VMEM_LIMIT),
        name="norm_inproj",
    )(h2d, g, w_all, colscale, cos2, sin2, _head_mean_matrix())


def _sb_kernel(q_ref, k_ref, v_ref, z_ref, r_ref, o_ref, acc_ref, carry_ref,
               *, tq, hpg, diag_block, sweep_rows, sweep_keys):
    qi = pl.program_id(2)
    n_wide = sweep_keys // SUFFIX_BLOCK
    acc_ref[...] = jnp.zeros_like(acc_ref)
    carry_ref[...] = jnp.zeros_like(carry_ref)

    def log_terms(q, kt):
        z = lax.dot_general(q, kt, (((1,), (1,)), ((), ())), preferred_element_type=F32)
        neg_part = jnp.minimum(z, 0.0)
        neg_pos = neg_part - z
        sp = jnp.log2(1.0 + jnp.exp2(neg_part + neg_pos))
        return neg_part - sp, neg_pos - sp

    def suffix(log_1m):
        n = log_1m.shape[1]
        return jnp.dot(log_1m.astype(BF16), r_ref[:n, :n], preferred_element_type=F32)

    for d in range(tq // diag_block - 1, -1, -1):
        r0 = d * diag_block
        ks = pl.multiple_of(qi * tq + r0, diag_block)
        for g in range(hpg):
            log_beta, log_1m = log_terms(q_ref[0, g, r0:, :],
                                         k_ref[0, g, pl.ds(ks, diag_block), :])
            shape = log_beta.shape
            mask = (lax.broadcasted_iota(jnp.int32, shape, 1)
                    < lax.broadcasted_iota(jnp.int32, shape, 0))
            log_1m = jnp.where(mask, log_1m, 0.0)
            carry = carry_ref[g, r0:, :]
            tot = (log_beta + suffix(log_1m)
                   + jnp.concatenate([carry] * (diag_block // LANES), axis=1))
            attn = jnp.where(mask, jnp.exp2(tot), 0.0)
            acc_ref[g, r0:, :] += jnp.dot(attn.astype(BF16),
                                          v_ref[0, g, pl.ds(ks, diag_block), :],
                                          preferred_element_type=F32)
            carry_ref[g, r0:, :] = carry + jnp.sum(log_1m, axis=1, keepdims=True)

    def sweep(rows):
        def still_visible():
            return jnp.max(carry_ref[:, rows, :]) > EXP2_UNDERFLOW

        n_steps = qi * (tq // sweep_keys)

        def cond(c):
            it, go = c
            return (it < n_steps) & go

        def body(c):
            it, _ = c
            ks = pl.multiple_of((n_steps - 1 - it) * sweep_keys, sweep_keys)
            for g in range(hpg):
                log_beta, log_1m = log_terms(q_ref[0, g, rows, :],
                                             k_ref[0, g, pl.ds(ks, sweep_keys), :])
                base = carry_ref[g, rows, :]
                parts = [None] * n_wide
                for s in range(n_wide - 1, -1, -1):
                    sl = slice(s * SUFFIX_BLOCK, (s + 1) * SUFFIX_BLOCK)
                    tot = (log_beta[:, sl] + suffix(log_1m[:, sl])
                           + jnp.concatenate([base] * (SUFFIX_BLOCK // LANES), axis=1))
                    parts[s] = jnp.exp2(tot).astype(BF16)
                    base = base + jnp.sum(log_1m[:, sl], axis=1, keepdims=True)
                carry_ref[g, rows, :] = base
                acc_ref[g, rows, :] += jnp.dot(jnp.concatenate(parts, axis=1),
                                               v_ref[0, g, pl.ds(ks, sweep_keys), :],
                                               preferred_element_type=F32)
            return it + 1, still_visible()

        lax.while_loop(cond, body, (jnp.int32(0), still_visible()))

    for r0 in range(0, tq, sweep_rows):
        sweep(slice(r0, r0 + sweep_rows))

    for g in range(hpg):
        zg = z_ref[0, g].astype(F32)
        silu = zg / (1.0 + jnp.exp(-zg))
        o_ref[0, g] = (acc_ref[g] * silu).astype(BF16)


def _sb_attention(proj, rmat, *, n_heads, tq=512, hpg=8, diag_block=SUFFIX_BLOCK,
                  sweep_rows=256, sweep_keys=SUFFIX_BLOCK):
    batch, _, seq, _ = proj.shape
    tq = min(tq, seq)
    n_groups = n_heads // hpg
    kern = functools.partial(_sb_kernel, tq=tq, hpg=hpg, diag_block=diag_block,
                             sweep_rows=sweep_rows, sweep_keys=sweep_keys)
    return pl.pallas_call(
        kern,
        grid=(batch, n_groups, seq // tq),
        in_specs=[
            pl.BlockSpec((1, hpg, tq, LANES), lambda b, h, i: (b, h, i, 0)),
            pl.BlockSpec((1, hpg, seq, LANES), lambda b, h, i: (b, n_groups + h, 0, 0)),
            pl.BlockSpec((1, hpg, seq, LANES), lambda b, h, i: (b, 2 * n_groups + h, 0, 0)),
            pl.BlockSpec((1, hpg, tq, LANES), lambda b, h, i: (b, 3 * n_groups + h, i, 0)),
            pl.BlockSpec(rmat.shape, lambda b, h, i: (0, 0)),
        ],
        out_specs=pl.BlockSpec((1, hpg, tq, LANES), lambda b, h, i: (b, h, i, 0)),
        out_shape=jax.ShapeDtypeStruct((batch, n_heads, seq, LANES), BF16),
        scratch_shapes=[pltpu.VMEM((hpg, tq, LANES), F32), pltpu.VMEM((hpg, tq, LANES), F32)],
        compiler_params=pltpu.CompilerParams(
            dimension_semantics=("parallel", "parallel", "arbitrary"),
            vmem_limit_bytes=VMEM_LIMIT),
        name="sb_attention",
    )(proj, proj, proj, proj, rmat)


def _df_kernel(lamv_ref, gains_ref, q_ref, k_ref, v_ref, z_ref, sg_ref, o_ref,
               m_ref, l_ref, acc_ref, *, tq, hpg, lam_init):
    qi = pl.program_id(2)

    def widen(x, n):
        return x if n == 1 else jnp.concatenate([x] * n, axis=1)

    def scores(c, rows, ks, nk, mask):
        s = lax.dot_general(q_ref[0, c, rows, :], k_ref[0, c, pl.ds(ks, nk), :],
                            (((1,), (1,)), ((), ())), preferred_element_type=F32)
        return s if mask is None else jnp.where(mask, s, NEG_BIG)

    def values(c, ks, nk):
        hd = c // 2
        return jnp.concatenate([v_ref[0, 2 * hd, pl.ds(ks, nk), :],
                                v_ref[0, 2 * hd + 1, pl.ds(ks, nk), :]], axis=1)

    def causal_mask(r0, nrows, nk):
        return (lax.broadcasted_iota(jnp.int32, (nrows, nk), 1)
                <= r0 + lax.broadcasted_iota(jnp.int32, (nrows, nk), 0))

    def tile_online(r0, nrows, ks, nk, masked, first):
        del first
        rows = slice(r0, r0 + nrows)
        mask = causal_mask(r0, nrows, nk) if masked else None
        for c in range(2 * hpg):
            s = scores(c, rows, ks, nk, mask)
            m_prev = m_ref[c, rows, :]
            m_next = jnp.maximum(m_prev, jnp.max(s, axis=1, keepdims=True))
            p = jnp.exp2(s - widen(m_next, nk // LANES))
            alpha = jnp.exp2(m_prev - m_next)
            l_ref[c, rows, :] = alpha * l_ref[c, rows, :] + jnp.sum(p, axis=1, keepdims=True)
            m_ref[c, rows, :] = m_next
            acc_ref[c, rows, :] = acc_ref[c, rows, :] * widen(alpha, 2) + jnp.dot(
                p.astype(BF16), values(c, ks, nk), preferred_element_type=F32)

    def tile_bounded(r0, nrows, ks, nk, masked, first):
        rows = slice(r0, r0 + nrows)
        mask = causal_mask(r0, nrows, nk) if masked else None
        for c in range(2 * hpg):
            p = jnp.exp2(scores(c, rows, ks, nk, mask))
            psum = jnp.sum(p, axis=1, keepdims=True)
            pv = jnp.dot(p.astype(BF16), values(c, ks, nk), preferred_element_type=F32)
            if first:
                l_ref[c, rows, :] = jnp.broadcast_to(psum, (nrows, LANES))
                acc_ref[c, rows, :] = pv
            else:
                l_ref[c, rows, :] += psum
                acc_ref[c, rows, :] += pv

    def sweep(tile):
        ks0 = pl.multiple_of(qi * tq, tq)
        half = tq // 2
        tile(0, half, ks0, half, True, True)
        tile(half, half, ks0, tq, True, True)

        def body(kt, c):
            tile(0, tq, pl.multiple_of(kt * tq, tq), tq, False, False)
            return c

        lax.fori_loop(0, qi, body, 0)

    gains = jnp.abs(gains_ref[...])
    bound = (jnp.max(gains[0:1], axis=-1, keepdims=True)
             * jnp.max(gains[1:2], axis=-1, keepdims=True) * (HEAD_DIM * NORM_SLACK))
    bounded = jnp.max(bound) <= SAFE_EXP2_RANGE

    @pl.when(bounded)
    def _():
        sweep(tile_bounded)

    @pl.when(jnp.logical_not(bounded))
    def _():
        m_ref[...] = jnp.full_like(m_ref, NEG_BIG)
        l_ref[...] = jnp.zeros_like(l_ref)
        acc_ref[...] = jnp.zeros_like(acc_ref)
        sweep(tile_online)

    lamv = lamv_ref[...]
    lam = (jnp.exp(jnp.sum(lamv[0:1] * lamv[1:2], axis=-1, keepdims=True))
           - jnp.exp(jnp.sum(lamv[2:3] * lamv[3:4], axis=-1, keepdims=True))
           + lam_init)
    out_gain = sg_ref[...] * (1.0 - lam_init)
    for hd in range(hpg):
        inv1 = 1.0 / l_ref[2 * hd]
        inv2 = lam / l_ref[2 * hd + 1]
        o = acc_ref[2 * hd] * widen(inv1, 2) - acc_ref[2 * hd + 1] * widen(inv2, 2)
        ms = jnp.mean(o * o, axis=-1, keepdims=True)
        o = o * lax.rsqrt(ms + EPS) * out_gain
        zg = jnp.concatenate([z_ref[0, 2 * hd], z_ref[0, 2 * hd + 1]], axis=1).astype(F32)
        y = o * (zg / (1.0 + jnp.exp(-zg)))
        o_ref[0, 2 * hd] = y[:, :LANES].astype(BF16)
        o_ref[0, 2 * hd + 1] = y[:, LANES:].astype(BF16)


def _df_attention(proj, lamv, qk_gains, sub_g, *, n_heads, lam_init, tq=512, hpg=4):
    batch, _, seq, _ = proj.shape
    tq = min(tq, seq)
    n_groups = n_heads // hpg
    nc = 2 * hpg
    kern = functools.partial(_df_kernel, tq=tq, hpg=hpg, lam_init=lam_init)
    return pl.pallas_call(
        kern,
        grid=(batch, n_groups, seq // tq),
        in_specs=[
            pl.BlockSpec((4, LANES), lambda b, h, i: (0, 0)),
            pl.BlockSpec((2, LANES), lambda b, h, i: (0, 0)),
            pl.BlockSpec((1, nc, tq, LANES), lambda b, h, i: (b, h, i, 0)),
            pl.BlockSpec((1, nc, seq, LANES), lambda b, h, i: (b, n_groups + h, 0, 0)),
            pl.BlockSpec((1, nc, seq, LANES), lambda b, h, i: (b, 2 * n_groups + h, 0, 0)),
            pl.BlockSpec((1, nc, tq, LANES), lambda b, h, i: (b, 3 * n_groups + h, i, 0)),
            pl.BlockSpec((1, 2 * LANES), lambda b, h, i: (0, 0)),
        ],
        out_specs=pl.BlockSpec((1, nc, tq, LANES), lambda b, h, i: (b, h, i, 0)),
        out_shape=jax.ShapeDtypeStruct((batch, 2 * n_heads, seq, LANES), BF16),
        scratch_shapes=[pltpu.VMEM((nc, tq, LANES), F32), pltpu.VMEM((nc, tq, LANES), F32),
                        pltpu.VMEM((nc, tq, 2 * LANES), F32)],
        compiler_params=pltpu.CompilerParams(
            dimension_semantics=("parallel", "parallel", "arbitrary"),
            vmem_limit_bytes=VMEM_LIMIT),
        name="df_attention",
    )(lamv, qk_gains, proj, proj, proj, proj, sub_g)


def _outproj_kernel(y_ref, w_ref, h_ref, o_ref, wb_ref):
    @pl.when((pl.program_id(0) == 0) & (pl.program_id(1) == 0))
    def _():
        wb_ref[...] = w_ref[0].astype(BF16)

    n_chunks = y_ref.shape[1]
    y = jnp.concatenate([y_ref[0, c] for c in range(n_chunks)], axis=1)
    o_ref[...] = h_ref[...] + jnp.dot(y, wb_ref[...], preferred_element_type=F32)


def _outproj(y, w_all, layer, h2d, *, tm=512):
    batch, n_chunks, seq, _ = y.shape
    _, e, d = w_all.shape
    tm = min(tm, seq)
    spt = seq // tm
    return pl.pallas_call(
        _outproj_kernel,
        grid=(batch, spt),
        in_specs=[
            pl.BlockSpec((1, n_chunks, tm, LANES), lambda b, i: (b, 0, i, 0)),
            pl.BlockSpec((1, e, d), lambda b, i: (layer, 0, 0)),
            pl.BlockSpec((tm, d), lambda b, i: (b * spt + i, 0)),
        ],
        out_specs=pl.BlockSpec((tm, d), lambda b, i: (b * spt + i, 0)),
        out_shape=jax.ShapeDtypeStruct(h2d.shape, F32),
        scratch_shapes=[pltpu.VMEM((e, d), BF16)],
        compiler_params=pltpu.CompilerParams(
            dimension_semantics=("arbitrary", "arbitrary"),
            vmem_limit_bytes=VMEM_LIMIT),
        name="outproj_residual",
    )(y, w_all, h2d)


def _rope_tables(seq):
    inv = 1.0 / (ROPE_THETA ** (jnp.arange(0, HEAD_DIM, 2, dtype=F32) / HEAD_DIM))
    ang = jnp.arange(seq, dtype=F32)[:, None] * inv[None, :]
    cos, sin = jnp.cos(ang), jnp.sin(ang)
    return jnp.concatenate([cos, cos], axis=1), jnp.concatenate([-sin, sin], axis=1)


def _suffix_matrix(n):
    j = jnp.arange(n)[:, None]
    s = jnp.arange(n)[None, :]
    return (j > s).astype(BF16)


def kernel(x, sb_norm, sb_w_in, sb_w_out, df_norm, df_w_in, df_w_out, df_q_norm, df_k_norm,
           df_lam_q1, df_lam_k1, df_lam_q2, df_lam_k2, df_sub_norm):
    batch, seq, d_model = x.shape
    e_width = sb_w_out.shape[1]
    depth = sb_norm.shape[0] + df_norm.shape[0]
    sb_heads = e_width // HEAD_DIM
    df_heads = e_width // (2 * HEAD_DIM)
    scale = 1.0 / math.sqrt(HEAD_DIM)

    cos2, sin2 = _rope_tables(seq)
    rmat = _suffix_matrix(SUFFIX_BLOCK)
    ones_e = jnp.ones((e_width,), F32)
    sb_colscale = jnp.concatenate([ones_e * (scale * LOG2E), ones_e, ones_e, ones_e])[None, :]

    h = x.reshape(batch * seq, d_model)
    for i in range(depth):
        j = i // N_MIXERS
        if i % N_MIXERS == 0:
            proj = _inproj(h, sb_norm[j][None, :], sb_w_in, j, sb_colscale,
                           cos2, sin2, batch=batch, seq=seq, n_rope_cols=0)
            y = _sb_attention(proj, rmat, n_heads=sb_heads)
            h = _outproj(y, sb_w_out, j, h)
        else:
            lam_init = 0.8 - 0.6 * math.exp(-0.3 * i)
            colscale = jnp.concatenate([jnp.tile(df_q_norm[j], 2 * df_heads) * (scale * LOG2E),
                                        jnp.tile(df_k_norm[j], 2 * df_heads),
                                        ones_e, ones_e])[None, :]
            proj = _inproj(h, df_norm[j][None, :], df_w_in, j, colscale,
                           cos2, sin2, batch=batch, seq=seq,
                           n_rope_cols=2 * e_width)
            lamv = jnp.stack([df_lam_q1[j], df_lam_k1[j], df_lam_q2[j], df_lam_k2[j]])
            qk_gains = jnp.stack([df_q_norm[j] * (scale * LOG2E), df_k_norm[j]])
            y = _df_attention(proj, lamv, qk_gains, df_sub_norm[j][None, :], n_heads=df_heads,
                              lam_init=lam_init)
            h = _outproj(y, df_w_out, j, h)
    return h.reshape(batch, seq, d_model)
```

```python
import functools
import math

import jax
import jax.numpy as jnp
from jax import lax
from jax.experimental import pallas as pl
from jax.experimental.pallas import tpu as pltpu

HEAD_DIM = 128
LANES = 128
EPS = 1e-6
ROPE_THETA = 10000.0
N_MIXERS = 2
NEG_BIG = -1e30
VMEM_LIMIT = 56 * 1024 * 1024
MXU_WIDTH = 256
SUFFIX_BLOCK = MXU_WIDTH
LOG2E = 1.4426950408889634
EXP2_UNDERFLOW = -160.0
SAFE_EXP2_RANGE = 50.0
NORM_SLACK = 1.02

F32 = jnp.float32
BF16 = jnp.bfloat16


def _inproj_kernel(x_ref, g_ref, w_ref, cs_ref, cos_ref, sin_ref, hm_ref, o_ref, xn_ref,
                   *, n_rope_tiles):
    j = pl.program_id(1)

    @pl.when(j == 0)
    def _():
        x = x_ref[...]
        ms = jnp.mean(x * x, axis=-1, keepdims=True)
        xn_ref[...] = (x * lax.rsqrt(ms + EPS) * g_ref[...]).astype(BF16)

    acc = jnp.dot(xn_ref[...], w_ref[0].astype(BF16), preferred_element_type=F32)
    n_chunks = acc.shape[1] // LANES

    def plain():
        for c in range(n_chunks):
            sl = slice(c * LANES, (c + 1) * LANES)
            o_ref[0, c] = (acc[:, sl] * cs_ref[:, sl]).astype(BF16)

    def qk_norm_rope():
        cos = cos_ref[...]
        sin = sin_ref[...]
        heads_per_slab = MXU_WIDTH // HEAD_DIM
        for s in range(acc.shape[1] // MXU_WIDTH):
            sl = slice(s * MXU_WIDTH, (s + 1) * MXU_WIDTH)
            xs = acc[:, sl]
            ms = jnp.dot((xs * xs).astype(BF16), hm_ref[...], preferred_element_type=F32)
            y = xs * lax.rsqrt(ms + EPS) * cs_ref[:, sl]
            for c in range(heads_per_slab):
                yc = y[:, c * HEAD_DIM:(c + 1) * HEAD_DIM]
                yc = yc * cos + pltpu.roll(yc, HEAD_DIM // 2, 1) * sin
                o_ref[0, s * heads_per_slab + c] = yc.astype(BF16)

    if n_rope_tiles == 0:
        plain()
    else:
        pl.when(j < n_rope_tiles)(qk_norm_rope)
        pl.when(j >= n_rope_tiles)(plain)


def _head_mean_matrix():
    head = jnp.arange(MXU_WIDTH) // HEAD_DIM
    return jnp.where(head[:, None] == head[None, :], 1.0 / HEAD_DIM, 0.0).astype(BF16)


def _inproj(h2d, g, w_all, layer, colscale, cos2, sin2, *, batch, seq, n_rope_cols,
            tm=2048, tn=1024):
    m, d = h2d.shape
    n = w_all.shape[2]
    tm = min(tm, seq)
    spt = seq // tm
    kern = functools.partial(_inproj_kernel, n_rope_tiles=n_rope_cols // tn)
    return pl.pallas_call(
        kern,
        grid=(m // tm, n // tn),
        in_specs=[
            pl.BlockSpec((tm, d), lambda i, j: (i, 0)),
            pl.BlockSpec((1, d), lambda i, j: (0, 0)),
            pl.BlockSpec((1, d, tn), lambda i, j: (layer, 0, j)),
            pl.BlockSpec((1, tn), lambda i, j: (0, j)),
            pl.BlockSpec((tm, LANES), lambda i, j: (i % spt, 0)),
            pl.BlockSpec((tm, LANES), lambda i, j: (i % spt, 0)),
            pl.BlockSpec((MXU_WIDTH, MXU_WIDTH), lambda i, j: (0, 0)),
        ],
        out_specs=pl.BlockSpec((1, tn // LANES, tm, LANES),
                               lambda i, j: (i // spt, j, i % spt, 0)),
        out_shape=jax.ShapeDtypeStruct((batch, n // LANES, seq, LANES), BF16),
        scratch_shapes=[pltpu.VMEM((tm, d), BF16)],
        compiler_params=pltpu.CompilerParams(
            dimension_semantics=("parallel", "arbitrary"),
            vmem_limit_bytes=VMEM_LIMIT),
        name="norm_inproj",
    )(h2d, g, w_all, colscale, cos2, sin2, _head_mean_matrix())


def _sb_kernel(q_ref, k_ref, v_ref, z_ref, r_ref, o_ref, acc_ref, carry_ref,
               *, tq, hpg, diag_block, sweep_rows, sweep_keys):
    qi = pl.program_id(2)
    n_wide = sweep_keys // SUFFIX_BLOCK
    acc_ref[...] = jnp.zeros_like(acc_ref)
    carry_ref[...] = jnp.zeros_like(carry_ref)

    def log_terms(q, kt):
        z = lax.dot_general(q, kt, (((1,), (1,)), ((), ())), preferred_element_type=F32)
        neg_part = jnp.minimum(z, 0.0)
        neg_pos = neg_part - z
        sp = jnp.log2(1.0 + jnp.exp2(neg_part + neg_pos))
        return neg_part - sp, neg_pos - sp

    def suffix(log_1m):
        n = log_1m.shape[1]
        return jnp.dot(log_1m.astype(BF16), r_ref[:n, :n], preferred_element_type=F32)

    for d in range(tq // diag_block - 1, -1, -1):
        r0 = d * diag_block
        ks = pl.multiple_of(qi * tq + r0, diag_block)
        for g in range(hpg):
            log_beta, log_1m = log_terms(q_ref[0, g, r0:, :],
                                         k_ref[0, g, pl.ds(ks, diag_block), :])
            shape = log_beta.shape
            mask = (lax.broadcasted_iota(jnp.int32, shape, 1)
                    < lax.broadcasted_iota(jnp.int32, shape, 0))
            log_1m = jnp.where(mask, log_1m, 0.0)
            carry = carry_ref[g, r0:, :]
            tot = (log_beta + suffix(log_1m)
                   + jnp.concatenate([carry] * (diag_block // LANES), axis=1))
            attn = jnp.where(mask, jnp.exp2(tot), 0.0)
            acc_ref[g, r0:, :] += jnp.dot(attn.astype(BF16),
                                          v_ref[0, g, pl.ds(ks, diag_block), :],
                                          preferred_element_type=F32)
            carry_ref[g, r0:, :] = carry + jnp.sum(log_1m, axis=1, keepdims=True)

    def sweep(rows):
        def still_visible():
            return jnp.max(carry_ref[:, rows, :]) > EXP2_UNDERFLOW

        n_steps = qi * (tq // sweep_keys)

        def cond(c):
            it, go = c
            return (it < n_steps) & go

        def body(c):
            it, _ = c
            ks = pl.multiple_of((n_steps - 1 - it) * sweep_keys, sweep_keys)
            for g in range(hpg):
                log_beta, log_1m = log_terms(q_ref[0, g, rows, :],
                                             k_ref[0, g, pl.ds(ks, sweep_keys), :])
                base = carry_ref[g, rows, :]
                parts = [None] * n_wide
                for s in range(n_wide - 1, -1, -1):
                    sl = slice(s * SUFFIX_BLOCK, (s + 1) * SUFFIX_BLOCK)
                    tot = (log_beta[:, sl] + suffix(log_1m[:, sl])
                           + jnp.concatenate([base] * (SUFFIX_BLOCK // LANES), axis=1))
                    parts[s] = jnp.exp2(tot).astype(BF16)
                    base = base + jnp.sum(log_1m[:, sl], axis=1, keepdims=True)
                carry_ref[g, rows, :] = base
                acc_ref[g, rows, :] += jnp.dot(jnp.concatenate(parts, axis=1),
                                               v_ref[0, g, pl.ds(ks, sweep_keys), :],
                                               preferred_element_type=F32)
            return it + 1, still_visible()

        lax.while_loop(cond, body, (jnp.int32(0), still_visible()))

    for r0 in range(0, tq, sweep_rows):
        sweep(slice(r0, r0 + sweep_rows))

    for g in range(hpg):
        zg = z_ref[0, g].astype(F32)
        silu = zg / (1.0 + jnp.exp(-zg))
        o_ref[0, g] = (acc_ref[g] * silu).astype(BF16)


def _sb_attention(proj, rmat, *, n_heads, tq=512, hpg=8, diag_block=SUFFIX_BLOCK,
                  sweep_rows=256, sweep_keys=SUFFIX_BLOCK):
    batch, _, seq, _ = proj.shape
    tq = min(tq, seq)
    n_groups = n_heads // hpg
    kern = functools.partial(_sb_kernel, tq=tq, hpg=hpg, diag_block=diag_block,
                             sweep_rows=sweep_rows, sweep_keys=sweep_keys)
    return pl.pallas_call(
        kern,
        grid=(batch, n_groups, seq // tq),
        in_specs=[
            pl.BlockSpec((1, hpg, tq, LANES), lambda b, h, i: (b, h, i, 0)),
            pl.BlockSpec((1, hpg, seq, LANES), lambda b, h, i: (b, n_groups + h, 0, 0)),
            pl.BlockSpec((1, hpg, seq, LANES), lambda b, h, i: (b, 2 * n_groups + h, 0, 0)),
            pl.BlockSpec((1, hpg, tq, LANES), lambda b, h, i: (b, 3 * n_groups + h, i, 0)),
            pl.BlockSpec(rmat.shape, lambda b, h, i: (0, 0)),
        ],
        out_specs=pl.BlockSpec((1, hpg, tq, LANES), lambda b, h, i: (b, h, i, 0)),
        out_shape=jax.ShapeDtypeStruct((batch, n_heads, seq, LANES), BF16),
        scratch_shapes=[pltpu.VMEM((hpg, tq, LANES), F32), pltpu.VMEM((hpg, tq, LANES), F32)],
        compiler_params=pltpu.CompilerParams(
            dimension_semantics=("parallel", "parallel", "arbitrary"),
            vmem_limit_bytes=VMEM_LIMIT),
        name="sb_attention",
    )(proj, proj, proj, proj, rmat)


def _df_kernel(lamv_ref, gains_ref, q_ref, k_ref, v_ref, z_ref, sg_ref, o_ref,
               m_ref, l_ref, acc_ref, *, tq, hpg, lam_init):
    qi = pl.program_id(2)

    def widen(x, n):
        return x if n == 1 else jnp.concatenate([x] * n, axis=1)

    def scores(c, rows, ks, nk, mask):
        s = lax.dot_general(q_ref[0, c, rows, :], k_ref[0, c, pl.ds(ks, nk), :],
                            (((1,), (1,)), ((), ())), preferred_element_type=F32)
        return s if mask is None else jnp.where(mask, s, NEG_BIG)

    def values(c, ks, nk):
        hd = c // 2
        return jnp.concatenate([v_ref[0, 2 * hd, pl.ds(ks, nk), :],
                                v_ref[0, 2 * hd + 1, pl.ds(ks, nk), :]], axis=1)

    def causal_mask(r0, nrows, nk):
        return (lax.broadcasted_iota(jnp.int32, (nrows, nk), 1)
                <= r0 + lax.broadcasted_iota(jnp.int32, (nrows, nk), 0))

    def tile_online(r0, nrows, ks, nk, masked, first):
        del first
        rows = slice(r0, r0 + nrows)
        mask = causal_mask(r0, nrows, nk) if masked else None
        for c in range(2 * hpg):
            s = scores(c, rows, ks, nk, mask)
            m_prev = m_ref[c, rows, :]
            m_next = jnp.maximum(m_prev, jnp.max(s, axis=1, keepdims=True))
            p = jnp.exp2(s - widen(m_next, nk // LANES))
            alpha = jnp.exp2(m_prev - m_next)
            l_ref[c, rows, :] = alpha * l_ref[c, rows, :] + jnp.sum(p, axis=1, keepdims=True)
            m_ref[c, rows, :] = m_next
            acc_ref[c, rows, :] = acc_ref[c, rows, :] * widen(alpha, 2) + jnp.dot(
                p.astype(BF16), values(c, ks, nk), preferred_element_type=F32)

    def tile_bounded(r0, nrows, ks, nk, masked, first):
        rows = slice(r0, r0 + nrows)
        mask = causal_mask(r0, nrows, nk) if masked else None
        for c in range(2 * hpg):
            p = jnp.exp2(scores(c, rows, ks, nk, mask))
            psum = jnp.sum(p, axis=1, keepdims=True)
            pv = jnp.dot(p.astype(BF16), values(c, ks, nk), preferred_element_type=F32)
            if first:
                l_ref[c, rows, :] = jnp.broadcast_to(psum, (nrows, LANES))
                acc_ref[c, rows, :] = pv
            else:
                l_ref[c, rows, :] += psum
                acc_ref[c, rows, :] += pv

    def sweep(tile):
        ks0 = pl.multiple_of(qi * tq, tq)
        half = tq // 2
        tile(0, half, ks0, half, True, True)
        tile(half, half, ks0, tq, True, True)

        def body(kt, c):
            tile(0, tq, pl.multiple_of(kt * tq, tq), tq, False, False)
            return c

        lax.fori_loop(0, qi, body, 0)

    gains = jnp.abs(gains_ref[...])
    bound = (jnp.max(gains[0:1], axis=-1, keepdims=True)
             * jnp.max(gains[1:2], axis=-1, keepdims=True) * (HEAD_DIM * NORM_SLACK))
    bounded = jnp.max(bound) <= SAFE_EXP2_RANGE

    @pl.when(bounded)
    def _():
        sweep(tile_bounded)

    @pl.when(jnp.logical_not(bounded))
    def _():
        m_ref[...] = jnp.full_like(m_ref, NEG_BIG)
        l_ref[...] = jnp.zeros_like(l_ref)
        acc_ref[...] = jnp.zeros_like(acc_ref)
        sweep(tile_online)

    lamv = lamv_ref[...]
    lam = (jnp.exp(jnp.sum(lamv[0:1] * lamv[1:2], axis=-1, keepdims=True))
           - jnp.exp(jnp.sum(lamv[2:3] * lamv[3:4], axis=-1, keepdims=True))
           + lam_init)
    out_gain = sg_ref[...] * (1.0 - lam_init)
    for hd in range(hpg):
        inv1 = 1.0 / l_ref[2 * hd]
        inv2 = lam / l_ref[2 * hd + 1]
        o = acc_ref[2 * hd] * widen(inv1, 2) - acc_ref[2 * hd + 1] * widen(inv2, 2)
        ms = jnp.mean(o * o, axis=-1, keepdims=True)
        o = o * lax.rsqrt(ms + EPS) * out_gain
        zg = jnp.concatenate([z_ref[0, 2 * hd], z_ref[0, 2 * hd + 1]], axis=1).astype(F32)
        y = o * (zg / (1.0 + jnp.exp(-zg)))
        o_ref[0, 2 * hd] = y[:, :LANES].astype(BF16)
        o_ref[0, 2 * hd + 1] = y[:, LANES:].astype(BF16)


def _df_attention(proj, lamv, qk_gains, sub_g, *, n_heads, lam_init, tq=512, hpg=4):
    batch, _, seq, _ = proj.shape
    tq = min(tq, seq)
    n_groups = n_heads // hpg
    nc = 2 * hpg
    kern = functools.partial(_df_kernel, tq=tq, hpg=hpg, lam_init=lam_init)
    return pl.pallas_call(
        kern,
        grid=(batch, n_groups, seq // tq),
        in_specs=[
            pl.BlockSpec((4, LANES), lambda b, h, i: (0, 0)),
            pl.BlockSpec((2, LANES), lambda b, h, i: (0, 0)),
            pl.BlockSpec((1, nc, tq, LANES), lambda b, h, i: (b, h, i, 0)),
            pl.BlockSpec((1, nc, seq, LANES), lambda b, h, i: (b, n_groups + h, 0, 0)),
            pl.BlockSpec((1, nc, seq, LANES), lambda b, h, i: (b, 2 * n_groups + h, 0, 0)),
            pl.BlockSpec((1, nc, tq, LANES), lambda b, h, i: (b, 3 * n_groups + h, i, 0)),
            pl.BlockSpec((1, 2 * LANES), lambda b, h, i: (0, 0)),
        ],
        out_specs=pl.BlockSpec((1, nc, tq, LANES), lambda b, h, i: (b, h, i, 0)),
        out_shape=jax.ShapeDtypeStruct((batch, 2 * n_heads, seq, LANES), BF16),
        scratch_shapes=[pltpu.VMEM((nc, tq, LANES), F32), pltpu.VMEM((nc, tq, LANES), F32),
                        pltpu.VMEM((nc, tq, 2 * LANES), F32)],
        compiler_params=pltpu.CompilerParams(
            dimension_semantics=("parallel", "parallel", "arbitrary"),
            vmem_limit_bytes=VMEM_LIMIT),
        name="df_attention",
    )(lamv, qk_gains, proj, proj, proj, proj, sub_g)


def _outproj_kernel(y_ref, w_ref, h_ref, o_ref, wb_ref):
    @pl.when((pl.program_id(0) == 0) & (pl.program_id(1) == 0))
    def _():
        wb_ref[...] = w_ref[0].astype(BF16)

    n_chunks = y_ref.shape[1]
    y = jnp.concatenate([y_ref[0, c] for c in range(n_chunks)], axis=1)
    o_ref[...] = h_ref[...] + jnp.dot(y, wb_ref[...], preferred_element_type=F32)


def _outproj(y, w_all, layer, h2d, *, tm=1024):
    batch, n_chunks, seq, _ = y.shape
    _, e, d = w_all.shape
    tm = min(tm, seq)
    spt = seq // tm
    return pl.pallas_call(
        _outproj_kernel,
        grid=(batch, spt),
        in_specs=[
            pl.BlockSpec((1, n_chunks, tm, LANES), lambda b, i: (b, 0, i, 0)),
            pl.BlockSpec((1, e, d), lambda b, i: (layer, 0, 0)),
            pl.BlockSpec((tm, d), lambda b, i: (b * spt + i, 0)),
        ],
        out_specs=pl.BlockSpec((tm, d), lambda b, i: (b * spt + i, 0)),
        out_shape=jax.ShapeDtypeStruct(h2d.shape, F32),
        scratch_shapes=[pltpu.VMEM((e, d), BF16)],
        compiler_params=pltpu.CompilerParams(
            dimension_semantics=("arbitrary", "arbitrary"),
            vmem_limit_bytes=VMEM_LIMIT),
        name="outproj_residual",
    )(y, w_all, h2d)


def _rope_tables(seq):
    inv = 1.0 / (ROPE_THETA ** (jnp.arange(0, HEAD_DIM, 2, dtype=F32) / HEAD_DIM))
    ang = jnp.arange(seq, dtype=F32)[:, None] * inv[None, :]
    cos, sin = jnp.cos(ang), jnp.sin(ang)
    return jnp.concatenate([cos, cos], axis=1), jnp.concatenate([-sin, sin], axis=1)


def _suffix_matrix(n):
    j = jnp.arange(n)[:, None]
    s = jnp.arange(n)[None, :]
    return (j > s).astype(BF16)


def kernel(x, sb_norm, sb_w_in, sb_w_out, df_norm, df_w_in, df_w_out, df_q_norm, df_k_norm,
           df_lam_q1, df_lam_k1, df_lam_q2, df_lam_k2, df_sub_norm):
    batch, seq, d_model = x.shape
    e_width = sb_w_out.shape[1]
    depth = sb_norm.shape[0] + df_norm.shape[0]
    sb_heads = e_width // HEAD_DIM
    df_heads = e_width // (2 * HEAD_DIM)
    scale = 1.0 / math.sqrt(HEAD_DIM)

    cos2, sin2 = _rope_tables(seq)
    rmat = _suffix_matrix(SUFFIX_BLOCK)
    ones_e = jnp.ones((e_width,), F32)
    sb_colscale = jnp.concatenate([ones_e * (scale * LOG2E), ones_e, ones_e, ones_e])[None, :]

    h = x.reshape(batch * seq, d_model)
    for i in range(depth):
        j = i // N_MIXERS
        if i % N_MIXERS == 0:
            proj = _inproj(h, sb_norm[j][None, :], sb_w_in, j, sb_colscale,
                           cos2, sin2, batch=batch, seq=seq, n_rope_cols=0)
            y = _sb_attention(proj, rmat, n_heads=sb_heads)
            h = _outproj(y, sb_w_out, j, h)
        else:
            lam_init = 0.8 - 0.6 * math.exp(-0.3 * i)
            colscale = jnp.concatenate([jnp.tile(df_q_norm[j], 2 * df_heads) * (scale * LOG2E),
                                        jnp.tile(df_k_norm[j], 2 * df_heads),
                                        ones_e, ones_e])[None, :]
            proj = _inproj(h, df_norm[j][None, :], df_w_in, j, colscale,
                           cos2, sin2, batch=batch, seq=seq,
                           n_rope_cols=2 * e_width)
            lamv = jnp.stack([df_lam_q1[j], df_lam_k1[j], df_lam_q2[j], df_lam_k2[j]])
            qk_gains = jnp.stack([df_q_norm[j] * (scale * LOG2E), df_k_norm[j]])
            y = _df_attention(proj, lamv, qk_gains, df_sub_norm[j][None, :], n_heads=df_heads,
                              lam_init=lam_init)
            h = _outproj(y, df_w_out, j, h)
    return h.reshape(batch, seq, d_model)
```

```python
import functools
import math

import jax
import jax.numpy as jnp
from jax import lax
from jax.experimental import pallas as pl
from jax.experimental.pallas import tpu as pltpu

HEAD_DIM = 128
LANES = 128
EPS = 1e-6
ROPE_THETA = 10000.0
N_MIXERS = 2
NEG_BIG = -1e30
VMEM_LIMIT = 56 * 1024 * 1024
MXU_WIDTH = 256
SUFFIX_BLOCK = MXU_WIDTH
LOG2E = 1.4426950408889634
EXP2_UNDERFLOW = -160.0
SAFE_EXP2_RANGE = 50.0
NORM_SLACK = 1.02

F32 = jnp.float32
BF16 = jnp.bfloat16


def _silu(z):
    half = 0.5 * z
    return half + half * jnp.tanh(half)


def _inproj_kernel(x_ref, g_ref, w_ref, cs_ref, cos_ref, sin_ref, hm_ref, o_ref, xn_ref,
                   *, n_rope_tiles):
    j = pl.program_id(1)

    @pl.when(j == 0)
    def _():
        x = x_ref[...]
        ms = jnp.mean(x * x, axis=-1, keepdims=True)
        xn_ref[...] = (x * lax.rsqrt(ms + EPS) * g_ref[...]).astype(BF16)

    acc = jnp.dot(xn_ref[...], w_ref[0].astype(BF16), preferred_element_type=F32)
    n_chunks = acc.shape[1] // LANES

    def plain():
        for c in range(n_chunks):
            sl = slice(c * LANES, (c + 1) * LANES)
            o_ref[0, c] = (acc[:, sl] * cs_ref[:, sl]).astype(BF16)

    def qk_norm_rope():
        cos = cos_ref[...]
        sin = sin_ref[...]
        heads_per_slab = MXU_WIDTH // HEAD_DIM
        for s in range(acc.shape[1] // MXU_WIDTH):
            sl = slice(s * MXU_WIDTH, (s + 1) * MXU_WIDTH)
            xs = acc[:, sl]
            ms = jnp.dot((xs * xs).astype(BF16), hm_ref[...], preferred_element_type=F32)
            y = xs * lax.rsqrt(ms + EPS) * cs_ref[:, sl]
            for c in range(heads_per_slab):
                yc = y[:, c * HEAD_DIM:(c + 1) * HEAD_DIM]
                yc = yc * cos + pltpu.roll(yc, HEAD_DIM // 2, 1) * sin
                o_ref[0, s * heads_per_slab + c] = yc.astype(BF16)

    if n_rope_tiles == 0:
        plain()
    else:
        pl.when(j < n_rope_tiles)(qk_norm_rope)
        pl.when(j >= n_rope_tiles)(plain)


def _head_mean_matrix():
    head = jnp.arange(MXU_WIDTH) // HEAD_DIM
    return jnp.where(head[:, None] == head[None, :], 1.0 / HEAD_DIM, 0.0).astype(BF16)


def _inproj(h2d, g, w_all, layer, colscale, cos2, sin2, *, batch, seq, n_rope_cols,
            tm=2048, tn=1024):
    m, d = h2d.shape
    n = w_all.shape[2]
    tm = min(tm, seq)
    spt = seq // tm
    kern = functools.partial(_inproj_kernel, n_rope_tiles=n_rope_cols // tn)
    return pl.pallas_call(
        kern,
        grid=(m // tm, n // tn),
        in_specs=[
            pl.BlockSpec((tm, d), lambda i, j: (i, 0)),
            pl.BlockSpec((1, d), lambda i, j: (0, 0)),
            pl.BlockSpec((1, d, tn), lambda i, j: (layer, 0, j)),
            pl.BlockSpec((1, tn), lambda i, j: (0, j)),
            pl.BlockSpec((tm, LANES), lambda i, j: (i % spt, 0)),
            pl.BlockSpec((tm, LANES), lambda i, j: (i % spt, 0)),
            pl.BlockSpec((MXU_WIDTH, MXU_WIDTH), lambda i, j: (0, 0)),
        ],
        out_specs=pl.BlockSpec((1, tn // LANES, tm, LANES),
                               lambda i, j: (i // spt, j, i % spt, 0)),
        out_shape=jax.ShapeDtypeStruct((batch, n // LANES, seq, LANES), BF16),
        scratch_shapes=[pltpu.VMEM((tm, d), BF16)],
        compiler_params=pltpu.CompilerParams(
            dimension_semantics=("parallel", "arbitrary"),
            vmem_limit_bytes=VMEM_LIMIT),
        name="norm_inproj",
    )(h2d, g, w_all, colscale, cos2, sin2, _head_mean_matrix())


def _sb_kernel(q_ref, k_ref, v_ref, z_ref, r_ref, o_ref, acc_ref, carry_ref,
               *, tq, hpg, diag_block, sweep_rows, sweep_keys):
    qi = pl.program_id(2)
    n_wide = sweep_keys // SUFFIX_BLOCK
    acc_ref[...] = jnp.zeros_like(acc_ref)
    carry_ref[...] = jnp.zeros_like(carry_ref)

    def log_terms(q, kt):
        z = lax.dot_general(q, kt, (((1,), (1,)), ((), ())), preferred_element_type=F32)
        neg_part = jnp.minimum(z, 0.0)
        neg_pos = neg_part - z
        sp = jnp.log2(1.0 + jnp.exp2(neg_part + neg_pos))
        return neg_part - sp, neg_pos - sp

    def suffix(log_1m):
        n = log_1m.shape[1]
        return jnp.dot(log_1m.astype(BF16), r_ref[:n, :n], preferred_element_type=F32)

    for d in range(tq // diag_block - 1, -1, -1):
        r0 = d * diag_block
        ks = pl.multiple_of(qi * tq + r0, diag_block)
        for g in range(hpg):
            log_beta, log_1m = log_terms(q_ref[0, g, r0:, :],
                                         k_ref[0, g, pl.ds(ks, diag_block), :])
            shape = log_beta.shape
            mask = (lax.broadcasted_iota(jnp.int32, shape, 1)
                    < lax.broadcasted_iota(jnp.int32, shape, 0))
            log_1m = jnp.where(mask, log_1m, 0.0)
            carry = carry_ref[g, r0:, :]
            tot = (log_beta + suffix(log_1m)
                   + jnp.concatenate([carry] * (diag_block // LANES), axis=1))
            attn = jnp.where(mask, jnp.exp2(tot), 0.0)
            acc_ref[g, r0:, :] += jnp.dot(attn.astype(BF16),
                                          v_ref[0, g, pl.ds(ks, diag_block), :],
                                          preferred_element_type=F32)
            carry_ref[g, r0:, :] = carry + jnp.sum(log_1m, axis=1, keepdims=True)

    def sweep(rows):
        def still_visible():
            return jnp.max(carry_ref[:, rows, :]) > EXP2_UNDERFLOW

        n_steps = qi * (tq // sweep_keys)

        def cond(c):
            it, go = c
            return (it < n_steps) & go

        def body(c):
            it, _ = c
            ks = pl.multiple_of((n_steps - 1 - it) * sweep_keys, sweep_keys)
            for g in range(hpg):
                log_beta, log_1m = log_terms(q_ref[0, g, rows, :],
                                             k_ref[0, g, pl.ds(ks, sweep_keys), :])
                base = carry_ref[g, rows, :]
                parts = [None] * n_wide
                for s in range(n_wide - 1, -1, -1):
                    sl = slice(s * SUFFIX_BLOCK, (s + 1) * SUFFIX_BLOCK)
                    tot = (log_beta[:, sl] + suffix(log_1m[:, sl])
                           + jnp.concatenate([base] * (SUFFIX_BLOCK // LANES), axis=1))
                    parts[s] = jnp.exp2(tot).astype(BF16)
                    base = base + jnp.sum(log_1m[:, sl], axis=1, keepdims=True)
                carry_ref[g, rows, :] = base
                acc_ref[g, rows, :] += jnp.dot(jnp.concatenate(parts, axis=1),
                                               v_ref[0, g, pl.ds(ks, sweep_keys), :],
                                               preferred_element_type=F32)
            return it + 1, still_visible()

        lax.while_loop(cond, body, (jnp.int32(0), still_visible()))

    for r0 in range(0, tq, sweep_rows):
        sweep(slice(r0, r0 + sweep_rows))

    for g in range(hpg):
        zg = z_ref[0, g].astype(F32)
        silu = _silu(zg)
        o_ref[0, g] = (acc_ref[g] * silu).astype(BF16)


def _sb_attention(proj, rmat, *, n_heads, tq=512, hpg=8, diag_block=SUFFIX_BLOCK,
                  sweep_rows=256, sweep_keys=SUFFIX_BLOCK):
    batch, _, seq, _ = proj.shape
    tq = min(tq, seq)
    n_groups = n_heads // hpg
    kern = functools.partial(_sb_kernel, tq=tq, hpg=hpg, diag_block=diag_block,
                             sweep_rows=sweep_rows, sweep_keys=sweep_keys)
    return pl.pallas_call(
        kern,
        grid=(batch, n_groups, seq // tq),
        in_specs=[
            pl.BlockSpec((1, hpg, tq, LANES), lambda b, h, i: (b, h, i, 0)),
            pl.BlockSpec((1, hpg, seq, LANES), lambda b, h, i: (b, n_groups + h, 0, 0)),
            pl.BlockSpec((1, hpg, seq, LANES), lambda b, h, i: (b, 2 * n_groups + h, 0, 0)),
            pl.BlockSpec((1, hpg, tq, LANES), lambda b, h, i: (b, 3 * n_groups + h, i, 0)),
            pl.BlockSpec(rmat.shape, lambda b, h, i: (0, 0)),
        ],
        out_specs=pl.BlockSpec((1, hpg, tq, LANES), lambda b, h, i: (b, h, i, 0)),
        out_shape=jax.ShapeDtypeStruct((batch, n_heads, seq, LANES), BF16),
        scratch_shapes=[pltpu.VMEM((hpg, tq, LANES), F32), pltpu.VMEM((hpg, tq, LANES), F32)],
        compiler_params=pltpu.CompilerParams(
            dimension_semantics=("parallel", "parallel", "arbitrary"),
            vmem_limit_bytes=VMEM_LIMIT),
        name="sb_attention",
    )(proj, proj, proj, proj, rmat)


def _df_kernel(lamv_ref, gains_ref, q_ref, k_ref, v_ref, z_ref, sg_ref, o_ref,
               m_ref, l_ref, acc_ref, *, tq, hpg, lam_init):
    qi = pl.program_id(2)

    def widen(x, n):
        return x if n == 1 else jnp.concatenate([x] * n, axis=1)

    def scores(c, rows, ks, nk, mask):
        s = lax.dot_general(q_ref[0, c, rows, :], k_ref[0, c, pl.ds(ks, nk), :],
                            (((1,), (1,)), ((), ())), preferred_element_type=F32)
        return s if mask is None else jnp.where(mask, s, NEG_BIG)

    def values(c, ks, nk):
        hd = c // 2
        return jnp.concatenate([v_ref[0, 2 * hd, pl.ds(ks, nk), :],
                                v_ref[0, 2 * hd + 1, pl.ds(ks, nk), :]], axis=1)

    def causal_mask(r0, nrows, nk):
        return (lax.broadcasted_iota(jnp.int32, (nrows, nk), 1)
                <= r0 + lax.broadcasted_iota(jnp.int32, (nrows, nk), 0))

    def tile_online(r0, nrows, ks, nk, masked, first):
        del first
        rows = slice(r0, r0 + nrows)
        mask = causal_mask(r0, nrows, nk) if masked else None
        for c in range(2 * hpg):
            s = scores(c, rows, ks, nk, mask)
            m_prev = m_ref[c, rows, :]
            m_next = jnp.maximum(m_prev, jnp.max(s, axis=1, keepdims=True))
            p = jnp.exp2(s - widen(m_next, nk // LANES))
            alpha = jnp.exp2(m_prev - m_next)
            l_ref[c, rows, :] = alpha * l_ref[c, rows, :] + jnp.sum(p, axis=1, keepdims=True)
            m_ref[c, rows, :] = m_next
            acc_ref[c, rows, :] = acc_ref[c, rows, :] * widen(alpha, 2) + jnp.dot(
                p.astype(BF16), values(c, ks, nk), preferred_element_type=F32)

    def tile_bounded(r0, nrows, ks, nk, masked, first):
        rows = slice(r0, r0 + nrows)
        mask = causal_mask(r0, nrows, nk) if masked else None
        for c in range(2 * hpg):
            p = jnp.exp2(scores(c, rows, ks, nk, mask))
            psum = jnp.sum(p, axis=1, keepdims=True)
            pv = jnp.dot(p.astype(BF16), values(c, ks, nk), preferred_element_type=F32)
            if first:
                l_ref[c, rows, :] = jnp.broadcast_to(psum, (nrows, LANES))
                acc_ref[c, rows, :] = pv
            else:
                l_ref[c, rows, :] += psum
                acc_ref[c, rows, :] += pv

    def sweep(tile):
        ks0 = pl.multiple_of(qi * tq, tq)
        half = tq // 2
        tile(0, half, ks0, half, True, True)
        tile(half, half, ks0, tq, True, True)

        def body(kt, c):
            tile(0, tq, pl.multiple_of(kt * tq, tq), tq, False, False)
            return c

        lax.fori_loop(0, qi, body, 0)

    gains = jnp.abs(gains_ref[...])
    bound = (jnp.max(gains[0:1], axis=-1, keepdims=True)
             * jnp.max(gains[1:2], axis=-1, keepdims=True) * (HEAD_DIM * NORM_SLACK))
    bounded = jnp.max(bound) <= SAFE_EXP2_RANGE

    @pl.when(bounded)
    def _():
        sweep(tile_bounded)

    @pl.when(jnp.logical_not(bounded))
    def _():
        m_ref[...] = jnp.full_like(m_ref, NEG_BIG)
        l_ref[...] = jnp.zeros_like(l_ref)
        acc_ref[...] = jnp.zeros_like(acc_ref)
        sweep(tile_online)

    lamv = lamv_ref[...]
    lam = (jnp.exp(jnp.sum(lamv[0:1] * lamv[1:2], axis=-1, keepdims=True))
           - jnp.exp(jnp.sum(lamv[2:3] * lamv[3:4], axis=-1, keepdims=True))
           + lam_init)
    out_gain = sg_ref[...] * (1.0 - lam_init)
    for hd in range(hpg):
        inv1 = 1.0 / l_ref[2 * hd]
        inv2 = lam / l_ref[2 * hd + 1]
        o = acc_ref[2 * hd] * widen(inv1, 2) - acc_ref[2 * hd + 1] * widen(inv2, 2)
        ms = jnp.mean(o * o, axis=-1, keepdims=True)
        o = o * lax.rsqrt(ms + EPS) * out_gain
        zg = jnp.concatenate([z_ref[0, 2 * hd], z_ref[0, 2 * hd + 1]], axis=1).astype(F32)
        y = o * _silu(zg)
        o_ref[0, 2 * hd] = y[:, :LANES].astype(BF16)
        o_ref[0, 2 * hd + 1] = y[:, LANES:].astype(BF16)


def _df_attention(proj, lamv, qk_gains, sub_g, *, n_heads, lam_init, tq=512, hpg=4):
    batch, _, seq, _ = proj.shape
    tq = min(tq, seq)
    n_groups = n_heads // hpg
    nc = 2 * hpg
    kern = functools.partial(_df_kernel, tq=tq, hpg=hpg, lam_init=lam_init)
    return pl.pallas_call(
        kern,
        grid=(batch, n_groups, seq // tq),
        in_specs=[
            pl.BlockSpec((4, LANES), lambda b, h, i: (0, 0)),
            pl.BlockSpec((2, LANES), lambda b, h, i: (0, 0)),
            pl.BlockSpec((1, nc, tq, LANES), lambda b, h, i: (b, h, i, 0)),
            pl.BlockSpec((1, nc, seq, LANES), lambda b, h, i: (b, n_groups + h, 0, 0)),
            pl.BlockSpec((1, nc, seq, LANES), lambda b, h, i: (b, 2 * n_groups + h, 0, 0)),
            pl.BlockSpec((1, nc, tq, LANES), lambda b, h, i: (b, 3 * n_groups + h, i, 0)),
            pl.BlockSpec((1, 2 * LANES), lambda b, h, i: (0, 0)),
        ],
        out_specs=pl.BlockSpec((1, nc, tq, LANES), lambda b, h, i: (b, h, i, 0)),
        out_shape=jax.ShapeDtypeStruct((batch, 2 * n_heads, seq, LANES), BF16),
        scratch_shapes=[pltpu.VMEM((nc, tq, LANES), F32), pltpu.VMEM((nc, tq, LANES), F32),
                        pltpu.VMEM((nc, tq, 2 * LANES), F32)],
        compiler_params=pltpu.CompilerParams(
            dimension_semantics=("parallel", "parallel", "arbitrary"),
            vmem_limit_bytes=VMEM_LIMIT),
        name="df_attention",
    )(lamv, qk_gains, proj, proj, proj, proj, sub_g)


def _outproj_kernel(y_ref, w_ref, h_ref, o_ref, wb_ref):
    @pl.when((pl.program_id(0) == 0) & (pl.program_id(1) == 0))
    def _():
        wb_ref[...] = w_ref[0].astype(BF16)

    n_chunks = y_ref.shape[1]
    y = jnp.concatenate([y_ref[0, c] for c in range(n_chunks)], axis=1)
    o_ref[...] = h_ref[...] + jnp.dot(y, wb_ref[...], preferred_element_type=F32)


def _outproj(y, w_all, layer, h2d, *, tm=1024):
    batch, n_chunks, seq, _ = y.shape
    _, e, d = w_all.shape
    tm = min(tm, seq)
    spt = seq // tm
    return pl.pallas_call(
        _outproj_kernel,
        grid=(batch, spt),
        in_specs=[
            pl.BlockSpec((1, n_chunks, tm, LANES), lambda b, i: (b, 0, i, 0)),
            pl.BlockSpec((1, e, d), lambda b, i: (layer, 0, 0)),
            pl.BlockSpec((tm, d), lambda b, i: (b * spt + i, 0)),
        ],
        out_specs=pl.BlockSpec((tm, d), lambda b, i: (b * spt + i, 0)),
        out_shape=jax.ShapeDtypeStruct(h2d.shape, F32),
        scratch_shapes=[pltpu.VMEM((e, d), BF16)],
        compiler_params=pltpu.CompilerParams(
            dimension_semantics=("arbitrary", "arbitrary"),
            vmem_limit_bytes=VMEM_LIMIT),
        name="outproj_residual",
    )(y, w_all, h2d)


def _rope_tables(seq):
    inv = 1.0 / (ROPE_THETA ** (jnp.arange(0, HEAD_DIM, 2, dtype=F32) / HEAD_DIM))
    ang = jnp.arange(seq, dtype=F32)[:, None] * inv[None, :]
    cos, sin = jnp.cos(ang), jnp.sin(ang)
    return jnp.concatenate([cos, cos], axis=1), jnp.concatenate([-sin, sin], axis=1)


def _suffix_matrix(n):
    j = jnp.arange(n)[:, None]
    s = jnp.arange(n)[None, :]
    return (j > s).astype(BF16)


def kernel(x, sb_norm, sb_w_in, sb_w_out, df_norm, df_w_in, df_w_out, df_q_norm, df_k_norm,
           df_lam_q1, df_lam_k1, df_lam_q2, df_lam_k2, df_sub_norm):
    batch, seq, d_model = x.shape
    e_width = sb_w_out.shape[1]
    depth = sb_norm.shape[0] + df_norm.shape[0]
    sb_heads = e_width // HEAD_DIM
    df_heads = e_width // (2 * HEAD_DIM)
    scale = 1.0 / math.sqrt(HEAD_DIM)

    cos2, sin2 = _rope_tables(seq)
    rmat = _suffix_matrix(SUFFIX_BLOCK)
    ones_e = jnp.ones((e_width,), F32)
    sb_colscale = jnp.concatenate([ones_e * (scale * LOG2E), ones_e, ones_e, ones_e])[None, :]

    h = x.reshape(batch * seq, d_model)
    for i in range(depth):
        j = i // N_MIXERS
        if i % N_MIXERS == 0:
            proj = _inproj(h, sb_norm[j][None, :], sb_w_in, j, sb_colscale,
                           cos2, sin2, batch=batch, seq=seq, n_rope_cols=0)
            y = _sb_attention(proj, rmat, n_heads=sb_heads)
            h = _outproj(y, sb_w_out, j, h)
        else:
            lam_init = 0.8 - 0.6 * math.exp(-0.3 * i)
            colscale = jnp.concatenate([jnp.tile(df_q_norm[j], 2 * df_heads) * (scale * LOG2E),
                                        jnp.tile(df_k_norm[j], 2 * df_heads),
                                        ones_e, ones_e])[None, :]
            proj = _inproj(h, df_norm[j][None, :], df_w_in, j, colscale,
                           cos2, sin2, batch=batch, seq=seq,
                           n_rope_cols=2 * e_width)
            lamv = jnp.stack([df_lam_q1[j], df_lam_k1[j], df_lam_q2[j], df_lam_k2[j]])
            qk_gains = jnp.stack([df_q_norm[j] * (scale * LOG2E), df_k_norm[j]])
            y = _df_attention(proj, lamv, qk_gains, df_sub_norm[j][None, :], n_heads=df_heads,
                              lam_init=lam_init)
            h = _outproj(y, df_w_out, j, h)
    return h.reshape(batch, seq, d_model)
```

```python
import functools
import math

import jax
import jax.numpy as jnp
from jax import lax
from jax.experimental import pallas as pl
from jax.experimental.pallas import tpu as pltpu

HEAD_DIM = 128
LANES = 128
EPS = 1e-6
ROPE_THETA = 10000.0
N_MIXERS = 2
NEG_BIG = -1e30
VMEM_LIMIT = 56 * 1024 * 1024
MXU_WIDTH = 256
SUFFIX_BLOCK = MXU_WIDTH
LOG2E = 1.4426950408889634
EXP2_UNDERFLOW = -160.0
SAFE_EXP2_RANGE = 50.0
NORM_SLACK = 1.02

F32 = jnp.float32
BF16 = jnp.bfloat16


def _silu(z):
    half = 0.5 * z
    return half + half * jnp.tanh(half)


def _inproj_kernel(x_ref, g_ref, w_ref, cs_ref, cos_ref, sin_ref, hm_ref, o_ref, xn_ref,
                   acc_ref, *, n_rope_tiles):
    j = pl.program_id(1)

    @pl.when(j == 0)
    def _():
        x = x_ref[...]
        ms = jnp.mean(x * x, axis=-1, keepdims=True)
        xn_ref[...] = (x * lax.rsqrt(ms + EPS) * g_ref[...]).astype(BF16)

    heads_per_slab = MXU_WIDTH // HEAD_DIM
    n_slabs = w_ref.shape[2] // MXU_WIDTH

    def matmul(cols):
        return jnp.dot(xn_ref[...], w_ref[0, :, cols].astype(BF16), preferred_element_type=F32)

    def plain():
        for s in range(n_slabs):
            sl = slice(s * MXU_WIDTH, (s + 1) * MXU_WIDTH)
            y = matmul(sl) * cs_ref[:, sl]
            for c in range(heads_per_slab):
                o_ref[0, s * heads_per_slab + c] = (
                    y[:, c * HEAD_DIM:(c + 1) * HEAD_DIM].astype(BF16))

    def qk_matmul():
        acc_ref[...] = matmul(slice(None))

    def qk_norm_rope():
        cos = cos_ref[...]
        sin = sin_ref[...]
        for s in range(n_slabs):
            sl = slice(s * MXU_WIDTH, (s + 1) * MXU_WIDTH)
            xs = acc_ref[:, sl]
            ms = jnp.dot((xs * xs).astype(BF16), hm_ref[...], preferred_element_type=F32)
            y = xs * lax.rsqrt(ms + EPS) * cs_ref[:, sl]
            for c in range(heads_per_slab):
                yc = y[:, c * HEAD_DIM:(c + 1) * HEAD_DIM]
                yc = yc * cos + pltpu.roll(yc, HEAD_DIM // 2, 1) * sin
                o_ref[0, s * heads_per_slab + c] = yc.astype(BF16)

    if n_rope_tiles == 0:
        plain()
    else:
        pl.when(j < n_rope_tiles)(qk_matmul)
        pl.when(j >= n_rope_tiles)(plain)
        pl.when(j < n_rope_tiles)(qk_norm_rope)


def _head_mean_matrix():
    head = jnp.arange(MXU_WIDTH) // HEAD_DIM
    return jnp.where(head[:, None] == head[None, :], 1.0 / HEAD_DIM, 0.0).astype(BF16)


def _inproj(h2d, g, w_all, layer, colscale, cos2, sin2, *, batch, seq, n_rope_cols,
            tm=2048, tn=1024):
    m, d = h2d.shape
    n = w_all.shape[2]
    tm = min(tm, seq)
    spt = seq // tm
    kern = functools.partial(_inproj_kernel, n_rope_tiles=n_rope_cols // tn)
    return pl.pallas_call(
        kern,
        grid=(m // tm, n // tn),
        in_specs=[
            pl.BlockSpec((tm, d), lambda i, j: (i, 0)),
            pl.BlockSpec((1, d), lambda i, j: (0, 0)),
            pl.BlockSpec((1, d, tn), lambda i, j: (layer, 0, j)),
            pl.BlockSpec((1, tn), lambda i, j: (0, j)),
            pl.BlockSpec((tm, LANES), lambda i, j: (i % spt, 0)),
            pl.BlockSpec((tm, LANES), lambda i, j: (i % spt, 0)),
            pl.BlockSpec((MXU_WIDTH, MXU_WIDTH), lambda i, j: (0, 0)),
        ],
        out_specs=pl.BlockSpec((1, tn // LANES, tm, LANES),
                               lambda i, j: (i // spt, j, i % spt, 0)),
        out_shape=jax.ShapeDtypeStruct((batch, n // LANES, seq, LANES), BF16),
        scratch_shapes=[pltpu.VMEM((tm, d), BF16),
                        pltpu.VMEM((tm, tn) if n_rope_cols else (8, LANES), F32)],
        compiler_params=pltpu.CompilerParams(
            dimension_semantics=("parallel", "arbitrary"),
            vmem_limit_bytes=VMEM_LIMIT),
        name="norm_inproj",
    )(h2d, g, w_all, colscale, cos2, sin2, _head_mean_matrix())


def _sb_kernel(q_ref, k_ref, v_ref, z_ref, r_ref, o_ref, acc_ref, carry_ref,
               *, tq, hpg, diag_block, sweep_rows, sweep_keys):
    qi = pl.program_id(2)
    n_wide = sweep_keys // SUFFIX_BLOCK
    acc_ref[...] = jnp.zeros_like(acc_ref)
    carry_ref[...] = jnp.zeros_like(carry_ref)

    def log_terms(q, kt):
        z = lax.dot_general(q, kt, (((1,), (1,)), ((), ())), preferred_element_type=F32)
        neg_part = jnp.minimum(z, 0.0)
        neg_pos = neg_part - z
        sp = jnp.log2(1.0 + jnp.exp2(neg_part + neg_pos))
        return neg_part - sp, neg_pos - sp

    def suffix(log_1m):
        n = log_1m.shape[1]
        return jnp.dot(log_1m.astype(BF16), r_ref[:n, :n], preferred_element_type=F32)

    for d in range(tq // diag_block - 1, -1, -1):
        r0 = d * diag_block
        ks = pl.multiple_of(qi * tq + r0, diag_block)
        for g in range(hpg):
            log_beta, log_1m = log_terms(q_ref[0, g, r0:, :],
                                         k_ref[0, g, pl.ds(ks, diag_block), :])
            shape = log_beta.shape
            mask = (lax.broadcasted_iota(jnp.int32, shape, 1)
                    < lax.broadcasted_iota(jnp.int32, shape, 0))
            log_1m = jnp.where(mask, log_1m, 0.0)
            carry = carry_ref[g, r0:, :]
            tot = (log_beta + suffix(log_1m)
                   + jnp.concatenate([carry] * (diag_block // LANES), axis=1))
            attn = jnp.where(mask, jnp.exp2(tot), 0.0)
            acc_ref[g, r0:, :] += jnp.dot(attn.astype(BF16),
                                          v_ref[0, g, pl.ds(ks, diag_block), :],
                                          preferred_element_type=F32)
            carry_ref[g, r0:, :] = carry + jnp.sum(log_1m, axis=1, keepdims=True)

    def sweep(rows):
        def still_visible():
            return jnp.max(carry_ref[:, rows, :]) > EXP2_UNDERFLOW

        n_steps = qi * (tq // sweep_keys)

        def cond(c):
            it, go = c
            return (it < n_steps) & go

        def body(c):
            it, _ = c
            ks = pl.multiple_of((n_steps - 1 - it) * sweep_keys, sweep_keys)
            for g in range(hpg):
                log_beta, log_1m = log_terms(q_ref[0, g, rows, :],
                                             k_ref[0, g, pl.ds(ks, sweep_keys), :])
                base = carry_ref[g, rows, :]
                parts = [None] * n_wide
                for s in range(n_wide - 1, -1, -1):
                    sl = slice(s * SUFFIX_BLOCK, (s + 1) * SUFFIX_BLOCK)
                    tot = (log_beta[:, sl] + suffix(log_1m[:, sl])
                           + jnp.concatenate([base] * (SUFFIX_BLOCK // LANES), axis=1))
                    parts[s] = jnp.exp2(tot).astype(BF16)
                    base = base + jnp.sum(log_1m[:, sl], axis=1, keepdims=True)
                carry_ref[g, rows, :] = base
                acc_ref[g, rows, :] += jnp.dot(jnp.concatenate(parts, axis=1),
                                               v_ref[0, g, pl.ds(ks, sweep_keys), :],
                                               preferred_element_type=F32)
            return it + 1, still_visible()

        lax.while_loop(cond, body, (jnp.int32(0), still_visible()))

    for r0 in range(0, tq, sweep_rows):
        sweep(slice(r0, r0 + sweep_rows))

    for g in range(hpg):
        zg = z_ref[0, g].astype(F32)
        silu = _silu(zg)
        o_ref[0, g] = (acc_ref[g] * silu).astype(BF16)


def _sb_attention(proj, rmat, *, n_heads, tq=512, hpg=8, diag_block=SUFFIX_BLOCK,
                  sweep_rows=256, sweep_keys=SUFFIX_BLOCK):
    batch, _, seq, _ = proj.shape
    tq = min(tq, seq)
    n_groups = n_heads // hpg
    kern = functools.partial(_sb_kernel, tq=tq, hpg=hpg, diag_block=diag_block,
                             sweep_rows=sweep_rows, sweep_keys=sweep_keys)
    return pl.pallas_call(
        kern,
        grid=(batch, n_groups, seq // tq),
        in_specs=[
            pl.BlockSpec((1, hpg, tq, LANES), lambda b, h, i: (b, h, i, 0)),
            pl.BlockSpec((1, hpg, seq, LANES), lambda b, h, i: (b, n_groups + h, 0, 0)),
            pl.BlockSpec((1, hpg, seq, LANES), lambda b, h, i: (b, 2 * n_groups + h, 0, 0)),
            pl.BlockSpec((1, hpg, tq, LANES), lambda b, h, i: (b, 3 * n_groups + h, i, 0)),
            pl.BlockSpec(rmat.shape, lambda b, h, i: (0, 0)),
        ],
        out_specs=pl.BlockSpec((1, hpg, tq, LANES), lambda b, h, i: (b, h, i, 0)),
        out_shape=jax.ShapeDtypeStruct((batch, n_heads, seq, LANES), BF16),
        scratch_shapes=[pltpu.VMEM((hpg, tq, LANES), F32), pltpu.VMEM((hpg, tq, LANES), F32)],
        compiler_params=pltpu.CompilerParams(
            dimension_semantics=("parallel", "parallel", "arbitrary"),
            vmem_limit_bytes=VMEM_LIMIT),
        name="sb_attention",
    )(proj, proj, proj, proj, rmat)


def _df_kernel(lamv_ref, gains_ref, q_ref, k_ref, v_ref, z_ref, sg_ref, o_ref,
               m_ref, l_ref, acc_ref, *, tq, hpg, lam_init):
    qi = pl.program_id(2)

    def widen(x, n):
        return x if n == 1 else jnp.concatenate([x] * n, axis=1)

    def scores(c, rows, ks, nk, mask):
        s = lax.dot_general(q_ref[0, c, rows, :], k_ref[0, c, pl.ds(ks, nk), :],
                            (((1,), (1,)), ((), ())), preferred_element_type=F32)
        return s if mask is None else jnp.where(mask, s, NEG_BIG)

    def values(c, ks, nk):
        hd = c // 2
        return jnp.concatenate([v_ref[0, 2 * hd, pl.ds(ks, nk), :],
                                v_ref[0, 2 * hd + 1, pl.ds(ks, nk), :]], axis=1)

    def causal_mask(r0, nrows, nk):
        return (lax.broadcasted_iota(jnp.int32, (nrows, nk), 1)
                <= r0 + lax.broadcasted_iota(jnp.int32, (nrows, nk), 0))

    def tile_online(r0, nrows, ks, nk, masked, first):
        del first
        rows = slice(r0, r0 + nrows)
        mask = causal_mask(r0, nrows, nk) if masked else None
        for c in range(2 * hpg):
            s = scores(c, rows, ks, nk, mask)
            m_prev = m_ref[c, rows, :]
            m_next = jnp.maximum(m_prev, jnp.max(s, axis=1, keepdims=True))
            p = jnp.exp2(s - widen(m_next, nk // LANES))
            alpha = jnp.exp2(m_prev - m_next)
            l_ref[c, rows, :] = alpha * l_ref[c, rows, :] + jnp.sum(p, axis=1, keepdims=True)
            m_ref[c, rows, :] = m_next
            acc_ref[c, rows, :] = acc_ref[c, rows, :] * widen(alpha, 2) + jnp.dot(
                p.astype(BF16), values(c, ks, nk), preferred_element_type=F32)

    def tile_bounded(r0, nrows, ks, nk, masked, first):
        rows = slice(r0, r0 + nrows)
        mask = causal_mask(r0, nrows, nk) if masked else None
        for c in range(2 * hpg):
            p = jnp.exp2(scores(c, rows, ks, nk, mask))
            psum = jnp.sum(p, axis=1, keepdims=True)
            pv = jnp.dot(p.astype(BF16), values(c, ks, nk), preferred_element_type=F32)
            if first:
                l_ref[c, rows, :] = jnp.broadcast_to(psum, (nrows, LANES))
                acc_ref[c, rows, :] = pv
            else:
                l_ref[c, rows, :] += psum
                acc_ref[c, rows, :] += pv

    def sweep(tile):
        ks0 = pl.multiple_of(qi * tq, tq)
        half = tq // 2
        tile(0, half, ks0, half, True, True)
        tile(half, half, ks0, tq, True, True)

        def body(kt, c):
            tile(0, tq, pl.multiple_of(kt * tq, tq), tq, False, False)
            return c

        lax.fori_loop(0, qi, body, 0)

    gains = jnp.abs(gains_ref[...])
    bound = (jnp.max(gains[0:1], axis=-1, keepdims=True)
             * jnp.max(gains[1:2], axis=-1, keepdims=True) * (HEAD_DIM * NORM_SLACK))
    bounded = jnp.max(bound) <= SAFE_EXP2_RANGE

    @pl.when(bounded)
    def _():
        sweep(tile_bounded)

    @pl.when(jnp.logical_not(bounded))
    def _():
        m_ref[...] = jnp.full_like(m_ref, NEG_BIG)
        l_ref[...] = jnp.zeros_like(l_ref)
        acc_ref[...] = jnp.zeros_like(acc_ref)
        sweep(tile_online)

    lamv = lamv_ref[...]
    lam = (jnp.exp(jnp.sum(lamv[0:1] * lamv[1:2], axis=-1, keepdims=True))
           - jnp.exp(jnp.sum(lamv[2:3] * lamv[3:4], axis=-1, keepdims=True))
           + lam_init)
    out_gain = sg_ref[...] * (1.0 - lam_init)
    for hd in range(hpg):
        inv1 = 1.0 / l_ref[2 * hd]
        inv2 = lam / l_ref[2 * hd + 1]
        o = acc_ref[2 * hd] * widen(inv1, 2) - acc_ref[2 * hd + 1] * widen(inv2, 2)
        ms = jnp.mean(o * o, axis=-1, keepdims=True)
        o = o * lax.rsqrt(ms + EPS) * out_gain
        zg = jnp.concatenate([z_ref[0, 2 * hd], z_ref[0, 2 * hd + 1]], axis=1).astype(F32)
        y = o * _silu(zg)
        o_ref[0, 2 * hd] = y[:, :LANES].astype(BF16)
        o_ref[0, 2 * hd + 1] = y[:, LANES:].astype(BF16)


def _df_attention(proj, lamv, qk_gains, sub_g, *, n_heads, lam_init, tq=512, hpg=4):
    batch, _, seq, _ = proj.shape
    tq = min(tq, seq)
    n_groups = n_heads // hpg
    nc = 2 * hpg
    kern = functools.partial(_df_kernel, tq=tq, hpg=hpg, lam_init=lam_init)
    return pl.pallas_call(
        kern,
        grid=(batch, n_groups, seq // tq),
        in_specs=[
            pl.BlockSpec((4, LANES), lambda b, h, i: (0, 0)),
            pl.BlockSpec((2, LANES), lambda b, h, i: (0, 0)),
            pl.BlockSpec((1, nc, tq, LANES), lambda b, h, i: (b, h, i, 0)),
            pl.BlockSpec((1, nc, seq, LANES), lambda b, h, i: (b, n_groups + h, 0, 0)),
            pl.BlockSpec((1, nc, seq, LANES), lambda b, h, i: (b, 2 * n_groups + h, 0, 0)),
            pl.BlockSpec((1, nc, tq, LANES), lambda b, h, i: (b, 3 * n_groups + h, i, 0)),
            pl.BlockSpec((1, 2 * LANES), lambda b, h, i: (0, 0)),
        ],
        out_specs=pl.BlockSpec((1, nc, tq, LANES), lambda b, h, i: (b, h, i, 0)),
        out_shape=jax.ShapeDtypeStruct((batch, 2 * n_heads, seq, LANES), BF16),
        scratch_shapes=[pltpu.VMEM((nc, tq, LANES), F32), pltpu.VMEM((nc, tq, LANES), F32),
                        pltpu.VMEM((nc, tq, 2 * LANES), F32)],
        compiler_params=pltpu.CompilerParams(
            dimension_semantics=("parallel", "parallel", "arbitrary"),
            vmem_limit_bytes=VMEM_LIMIT),
        name="df_attention",
    )(lamv, qk_gains, proj, proj, proj, proj, sub_g)


def _outproj_kernel(y_ref, w_ref, h_ref, o_ref, wb_ref):
    @pl.when((pl.program_id(0) == 0) & (pl.program_id(1) == 0))
    def _():
        wb_ref[...] = w_ref[0].astype(BF16)

    n_chunks = y_ref.shape[1]
    y = jnp.concatenate([y_ref[0, c] for c in range(n_chunks)], axis=1)
    o_ref[...] = h_ref[...] + jnp.dot(y, wb_ref[...], preferred_element_type=F32)


def _outproj(y, w_all, layer, h2d, *, tm=1024):
    batch, n_chunks, seq, _ = y.shape
    _, e, d = w_all.shape
    tm = min(tm, seq)
    spt = seq // tm
    return pl.pallas_call(
        _outproj_kernel,
        grid=(batch, spt),
        in_specs=[
            pl.BlockSpec((1, n_chunks, tm, LANES), lambda b, i: (b, 0, i, 0)),
            pl.BlockSpec((1, e, d), lambda b, i: (layer, 0, 0)),
            pl.BlockSpec((tm, d), lambda b, i: (b * spt + i, 0)),
        ],
        out_specs=pl.BlockSpec((tm, d), lambda b, i: (b * spt + i, 0)),
        out_shape=jax.ShapeDtypeStruct(h2d.shape, F32),
        scratch_shapes=[pltpu.VMEM((e, d), BF16)],
        compiler_params=pltpu.CompilerParams(
            dimension_semantics=("arbitrary", "arbitrary"),
            vmem_limit_bytes=VMEM_LIMIT),
        name="outproj_residual",
    )(y, w_all, h2d)


def _rope_tables(seq):
    inv = 1.0 / (ROPE_THETA ** (jnp.arange(0, HEAD_DIM, 2, dtype=F32) / HEAD_DIM))
    ang = jnp.arange(seq, dtype=F32)[:, None] * inv[None, :]
    cos, sin = jnp.cos(ang), jnp.sin(ang)
    return jnp.concatenate([cos, cos], axis=1), jnp.concatenate([-sin, sin], axis=1)


def _suffix_matrix(n):
    j = jnp.arange(n)[:, None]
    s = jnp.arange(n)[None, :]
    return (j > s).astype(BF16)


def kernel(x, sb_norm, sb_w_in, sb_w_out, df_norm, df_w_in, df_w_out, df_q_norm, df_k_norm,
           df_lam_q1, df_lam_k1, df_lam_q2, df_lam_k2, df_sub_norm):
    batch, seq, d_model = x.shape
    e_width = sb_w_out.shape[1]
    depth = sb_norm.shape[0] + df_norm.shape[0]
    sb_heads = e_width // HEAD_DIM
    df_heads = e_width // (2 * HEAD_DIM)
    scale = 1.0 / math.sqrt(HEAD_DIM)

    cos2, sin2 = _rope_tables(seq)
    rmat = _suffix_matrix(SUFFIX_BLOCK)
    ones_e = jnp.ones((e_width,), F32)
    sb_colscale = jnp.concatenate([ones_e * (scale * LOG2E), ones_e, ones_e, ones_e])[None, :]

    h = x.reshape(batch * seq, d_model)
    for i in range(depth):
        j = i // N_MIXERS
        if i % N_MIXERS == 0:
            proj = _inproj(h, sb_norm[j][None, :], sb_w_in, j, sb_colscale,
                           cos2, sin2, batch=batch, seq=seq, n_rope_cols=0)
            y = _sb_attention(proj, rmat, n_heads=sb_heads)
            h = _outproj(y, sb_w_out, j, h)
        else:
            lam_init = 0.8 - 0.6 * math.exp(-0.3 * i)
            colscale = jnp.concatenate([jnp.tile(df_q_norm[j], 2 * df_heads) * (scale * LOG2E),
                                        jnp.tile(df_k_norm[j], 2 * df_heads),
                                        ones_e, ones_e])[None, :]
            proj = _inproj(h, df_norm[j][None, :], df_w_in, j, colscale,
                           cos2, sin2, batch=batch, seq=seq,
                           n_rope_cols=2 * e_width)
            lamv = jnp.stack([df_lam_q1[j], df_lam_k1[j], df_lam_q2[j], df_lam_k2[j]])
            qk_gains = jnp.stack([df_q_norm[j] * (scale * LOG2E), df_k_norm[j]])
            y = _df_attention(proj, lamv, qk_gains, df_sub_norm[j][None, :], n_heads=df_heads,
                              lam_init=lam_init)
            h = _outproj(y, df_w_out, j, h)
    return h.reshape(batch, seq, d_model)
```

```python
import functools
import math

import jax
import jax.numpy as jnp
from jax import lax
from jax.experimental import pallas as pl
from jax.experimental.pallas import tpu as pltpu

HEAD_DIM = 128
LANES = 128
EPS = 1e-6
ROPE_THETA = 10000.0
N_MIXERS = 2
NEG_BIG = -1e30
VMEM_LIMIT = 56 * 1024 * 1024
MXU_WIDTH = 256
SUFFIX_BLOCK = MXU_WIDTH
LOG2E = 1.4426950408889634
EXP2_UNDERFLOW = -160.0
SAFE_EXP2_RANGE = 50.0
NORM_SLACK = 1.02

F32 = jnp.float32
BF16 = jnp.bfloat16


def _silu(z):
    half = 0.5 * z
    return half + half * jnp.tanh(half)


def _inproj_kernel(x_ref, g_ref, w_ref, cs_ref, cos_ref, sin_ref, hm_ref, o_ref, xn_ref,
                   *, n_rope_tiles, prenormed):
    j = pl.program_id(1)

    if prenormed:
        lhs = x_ref[...]
    else:
        @pl.when(j == 0)
        def _():
            x = x_ref[...]
            ms = jnp.mean(x * x, axis=-1, keepdims=True)
            xn_ref[...] = (x * lax.rsqrt(ms + EPS) * g_ref[...]).astype(BF16)

        lhs = xn_ref[...]

    acc = jnp.dot(lhs, w_ref[0].astype(BF16), preferred_element_type=F32)
    n_chunks = acc.shape[1] // LANES

    def plain():
        for c in range(n_chunks):
            sl = slice(c * LANES, (c + 1) * LANES)
            o_ref[0, c] = (acc[:, sl] * cs_ref[:, sl]).astype(BF16)

    def qk_norm_rope():
        cos = cos_ref[...]
        sin = sin_ref[...]
        heads_per_slab = MXU_WIDTH // HEAD_DIM
        for s in range(acc.shape[1] // MXU_WIDTH):
            sl = slice(s * MXU_WIDTH, (s + 1) * MXU_WIDTH)
            xs = acc[:, sl]
            ms = jnp.dot((xs * xs).astype(BF16), hm_ref[...], preferred_element_type=F32)
            y = xs * lax.rsqrt(ms + EPS) * cs_ref[:, sl]
            for c in range(heads_per_slab):
                yc = y[:, c * HEAD_DIM:(c + 1) * HEAD_DIM]
                yc = yc * cos + pltpu.roll(yc, HEAD_DIM // 2, 1) * sin
                o_ref[0, s * heads_per_slab + c] = yc.astype(BF16)

    if n_rope_tiles == 0:
        plain()
    else:
        pl.when(j < n_rope_tiles)(qk_norm_rope)
        pl.when(j >= n_rope_tiles)(plain)


def _head_mean_matrix():
    head = jnp.arange(MXU_WIDTH) // HEAD_DIM
    return jnp.where(head[:, None] == head[None, :], 1.0 / HEAD_DIM, 0.0).astype(BF16)


def _inproj(h2d, g, w_all, layer, colscale, cos2, sin2, *, batch, seq, n_rope_cols,
            tm=2048, tn=1024):
    m, d = h2d.shape
    n = w_all.shape[2]
    tm = min(tm, seq)
    spt = seq // tm
    prenormed = h2d.dtype == BF16
    kern = functools.partial(_inproj_kernel, n_rope_tiles=n_rope_cols // tn,
                             prenormed=prenormed)
    return pl.pallas_call(
        kern,
        grid=(m // tm, n // tn),
        in_specs=[
            pl.BlockSpec((tm, d), lambda i, j: (i, 0)),
            pl.BlockSpec((1, d), lambda i, j: (0, 0)),
            pl.BlockSpec((1, d, tn), lambda i, j: (layer, 0, j)),
            pl.BlockSpec((1, tn), lambda i, j: (0, j)),
            pl.BlockSpec((tm, LANES), lambda i, j: (i % spt, 0)),
            pl.BlockSpec((tm, LANES), lambda i, j: (i % spt, 0)),
            pl.BlockSpec((MXU_WIDTH, MXU_WIDTH), lambda i, j: (0, 0)),
        ],
        out_specs=pl.BlockSpec((1, tn // LANES, tm, LANES),
                               lambda i, j: (i // spt, j, i % spt, 0)),
        out_shape=jax.ShapeDtypeStruct((batch, n // LANES, seq, LANES), BF16),
        scratch_shapes=[pltpu.VMEM((16, LANES) if prenormed else (tm, d), BF16)],
        compiler_params=pltpu.CompilerParams(
            dimension_semantics=("parallel", "arbitrary"),
            vmem_limit_bytes=VMEM_LIMIT),
        name="norm_inproj",
    )(h2d, g, w_all, colscale, cos2, sin2, _head_mean_matrix())


def _sb_kernel(q_ref, k_ref, v_ref, z_ref, r_ref, o_ref, acc_ref, carry_ref,
               *, tq, hpg, diag_block, sweep_rows, sweep_keys):
    qi = pl.program_id(2)
    n_wide = sweep_keys // SUFFIX_BLOCK
    acc_ref[...] = jnp.zeros_like(acc_ref)
    carry_ref[...] = jnp.zeros_like(carry_ref)

    def log_terms(q, kt):
        z = lax.dot_general(q, kt, (((1,), (1,)), ((), ())), preferred_element_type=F32)
        neg_part = jnp.minimum(z, 0.0)
        neg_pos = neg_part - z
        sp = jnp.log2(1.0 + jnp.exp2(neg_part + neg_pos))
        return neg_part - sp, neg_pos - sp

    def suffix(log_1m):
        n = log_1m.shape[1]
        return jnp.dot(log_1m.astype(BF16), r_ref[:n, :n], preferred_element_type=F32)

    for d in range(tq // diag_block - 1, -1, -1):
        r0 = d * diag_block
        ks = pl.multiple_of(qi * tq + r0, diag_block)
        for g in range(hpg):
            log_beta, log_1m = log_terms(q_ref[0, g, r0:, :],
                                         k_ref[0, g, pl.ds(ks, diag_block), :])
            shape = log_beta.shape
            mask = (lax.broadcasted_iota(jnp.int32, shape, 1)
                    < lax.broadcasted_iota(jnp.int32, shape, 0))
            log_1m = jnp.where(mask, log_1m, 0.0)
            carry = carry_ref[g, r0:, :]
            tot = (log_beta + suffix(log_1m)
                   + jnp.concatenate([carry] * (diag_block // LANES), axis=1))
            attn = jnp.where(mask, jnp.exp2(tot), 0.0)
            acc_ref[g, r0:, :] += jnp.dot(attn.astype(BF16),
                                          v_ref[0, g, pl.ds(ks, diag_block), :],
                                          preferred_element_type=F32)
            carry_ref[g, r0:, :] = carry + jnp.sum(log_1m, axis=1, keepdims=True)

    def sweep(rows):
        def still_visible():
            return jnp.max(carry_ref[:, rows, :]) > EXP2_UNDERFLOW

        n_steps = qi * (tq // sweep_keys)

        def cond(c):
            it, go = c
            return (it < n_steps) & go

        def body(c):
            it, _ = c
            ks = pl.multiple_of((n_steps - 1 - it) * sweep_keys, sweep_keys)
            for g in range(hpg):
                log_beta, log_1m = log_terms(q_ref[0, g, rows, :],
                                             k_ref[0, g, pl.ds(ks, sweep_keys), :])
                base = carry_ref[g, rows, :]
                parts = [None] * n_wide
                for s in range(n_wide - 1, -1, -1):
                    sl = slice(s * SUFFIX_BLOCK, (s + 1) * SUFFIX_BLOCK)
                    tot = (log_beta[:, sl] + suffix(log_1m[:, sl])
                           + jnp.concatenate([base] * (SUFFIX_BLOCK // LANES), axis=1))
                    parts[s] = jnp.exp2(tot).astype(BF16)
                    base = base + jnp.sum(log_1m[:, sl], axis=1, keepdims=True)
                carry_ref[g, rows, :] = base
                acc_ref[g, rows, :] += jnp.dot(jnp.concatenate(parts, axis=1),
                                               v_ref[0, g, pl.ds(ks, sweep_keys), :],
                                               preferred_element_type=F32)
            return it + 1, still_visible()

        lax.while_loop(cond, body, (jnp.int32(0), still_visible()))

    for r0 in range(0, tq, sweep_rows):
        sweep(slice(r0, r0 + sweep_rows))

    for g in range(hpg):
        zg = z_ref[0, g].astype(F32)
        silu = _silu(zg)
        o_ref[0, g] = (acc_ref[g] * silu).astype(BF16)


def _sb_attention(proj, rmat, *, n_heads, tq=512, hpg=8, diag_block=SUFFIX_BLOCK,
                  sweep_rows=256, sweep_keys=SUFFIX_BLOCK):
    batch, _, seq, _ = proj.shape
    tq = min(tq, seq)
    n_groups = n_heads // hpg
    kern = functools.partial(_sb_kernel, tq=tq, hpg=hpg, diag_block=diag_block,
                             sweep_rows=sweep_rows, sweep_keys=sweep_keys)
    return pl.pallas_call(
        kern,
        grid=(batch, n_groups, seq // tq),
        in_specs=[
            pl.BlockSpec((1, hpg, tq, LANES), lambda b, h, i: (b, h, i, 0)),
            pl.BlockSpec((1, hpg, seq, LANES), lambda b, h, i: (b, n_groups + h, 0, 0)),
            pl.BlockSpec((1, hpg, seq, LANES), lambda b, h, i: (b, 2 * n_groups + h, 0, 0)),
            pl.BlockSpec((1, hpg, tq, LANES), lambda b, h, i: (b, 3 * n_groups + h, i, 0)),
            pl.BlockSpec(rmat.shape, lambda b, h, i: (0, 0)),
        ],
        out_specs=pl.BlockSpec((1, hpg, tq, LANES), lambda b, h, i: (b, h, i, 0)),
        out_shape=jax.ShapeDtypeStruct((batch, n_heads, seq, LANES), BF16),
        scratch_shapes=[pltpu.VMEM((hpg, tq, LANES), F32), pltpu.VMEM((hpg, tq, LANES), F32)],
        compiler_params=pltpu.CompilerParams(
            dimension_semantics=("parallel", "parallel", "arbitrary"),
            vmem_limit_bytes=VMEM_LIMIT),
        name="sb_attention",
    )(proj, proj, proj, proj, rmat)


def _df_kernel(lamv_ref, gains_ref, q_ref, k_ref, v_ref, z_ref, sg_ref, o_ref,
               m_ref, l_ref, acc_ref, *, tq, hpg, lam_init):
    qi = pl.program_id(2)

    def widen(x, n):
        return x if n == 1 else jnp.concatenate([x] * n, axis=1)

    def scores(c, rows, ks, nk, mask):
        s = lax.dot_general(q_ref[0, c, rows, :], k_ref[0, c, pl.ds(ks, nk), :],
                            (((1,), (1,)), ((), ())), preferred_element_type=F32)
        return s if mask is None else jnp.where(mask, s, NEG_BIG)

    def values(c, ks, nk):
        hd = c // 2
        return jnp.concatenate([v_ref[0, 2 * hd, pl.ds(ks, nk), :],
                                v_ref[0, 2 * hd + 1, pl.ds(ks, nk), :]], axis=1)

    def causal_mask(r0, nrows, nk):
        return (lax.broadcasted_iota(jnp.int32, (nrows, nk), 1)
                <= r0 + lax.broadcasted_iota(jnp.int32, (nrows, nk), 0))

    def tile_online(r0, nrows, ks, nk, masked, first):
        del first
        rows = slice(r0, r0 + nrows)
        mask = causal_mask(r0, nrows, nk) if masked else None
        for c in range(2 * hpg):
            s = scores(c, rows, ks, nk, mask)
            m_prev = m_ref[c, rows, :]
            m_next = jnp.maximum(m_prev, jnp.max(s, axis=1, keepdims=True))
            p = jnp.exp2(s - widen(m_next, nk // LANES))
            alpha = jnp.exp2(m_prev - m_next)
            l_ref[c, rows, :] = alpha * l_ref[c, rows, :] + jnp.sum(p, axis=1, keepdims=True)
            m_ref[c, rows, :] = m_next
            acc_ref[c, rows, :] = acc_ref[c, rows, :] * widen(alpha, 2) + jnp.dot(
                p.astype(BF16), values(c, ks, nk), preferred_element_type=F32)

    def tile_bounded(r0, nrows, ks, nk, masked, first):
        rows = slice(r0, r0 + nrows)
        mask = causal_mask(r0, nrows, nk) if masked else None
        for c in range(2 * hpg):
            p = jnp.exp2(scores(c, rows, ks, nk, mask))
            psum = jnp.sum(p, axis=1, keepdims=True)
            pv = jnp.dot(p.astype(BF16), values(c, ks, nk), preferred_element_type=F32)
            if first:
                l_ref[c, rows, :] = jnp.broadcast_to(psum, (nrows, LANES))
                acc_ref[c, rows, :] = pv
            else:
                l_ref[c, rows, :] += psum
                acc_ref[c, rows, :] += pv

    def sweep(tile):
        ks0 = pl.multiple_of(qi * tq, tq)
        half = tq // 2
        tile(0, half, ks0, half, True, True)
        tile(half, half, ks0, tq, True, True)

        def body(kt, c):
            tile(0, tq, pl.multiple_of(kt * tq, tq), tq, False, False)
            return c

        lax.fori_loop(0, qi, body, 0)

    gains = jnp.abs(gains_ref[...])
    bound = (jnp.max(gains[0:1], axis=-1, keepdims=True)
             * jnp.max(gains[1:2], axis=-1, keepdims=True) * (HEAD_DIM * NORM_SLACK))
    bounded = jnp.max(bound) <= SAFE_EXP2_RANGE

    @pl.when(bounded)
    def _():
        sweep(tile_bounded)

    @pl.when(jnp.logical_not(bounded))
    def _():
        m_ref[...] = jnp.full_like(m_ref, NEG_BIG)
        l_ref[...] = jnp.zeros_like(l_ref)
        acc_ref[...] = jnp.zeros_like(acc_ref)
        sweep(tile_online)

    lamv = lamv_ref[...]
    lam = (jnp.exp(jnp.sum(lamv[0:1] * lamv[1:2], axis=-1, keepdims=True))
           - jnp.exp(jnp.sum(lamv[2:3] * lamv[3:4], axis=-1, keepdims=True))
           + lam_init)
    out_gain = sg_ref[...] * (1.0 - lam_init)
    for hd in range(hpg):
        inv1 = 1.0 / l_ref[2 * hd]
        inv2 = lam / l_ref[2 * hd + 1]
        o = acc_ref[2 * hd] * widen(inv1, 2) - acc_ref[2 * hd + 1] * widen(inv2, 2)
        ms = jnp.mean(o * o, axis=-1, keepdims=True)
        o = o * lax.rsqrt(ms + EPS) * out_gain
        zg = jnp.concatenate([z_ref[0, 2 * hd], z_ref[0, 2 * hd + 1]], axis=1).astype(F32)
        y = o * _silu(zg)
        o_ref[0, 2 * hd] = y[:, :LANES].astype(BF16)
        o_ref[0, 2 * hd + 1] = y[:, LANES:].astype(BF16)


def _df_attention(proj, lamv, qk_gains, sub_g, *, n_heads, lam_init, tq=512, hpg=4):
    batch, _, seq, _ = proj.shape
    tq = min(tq, seq)
    n_groups = n_heads // hpg
    nc = 2 * hpg
    kern = functools.partial(_df_kernel, tq=tq, hpg=hpg, lam_init=lam_init)
    return pl.pallas_call(
        kern,
        grid=(batch, n_groups, seq // tq),
        in_specs=[
            pl.BlockSpec((4, LANES), lambda b, h, i: (0, 0)),
            pl.BlockSpec((2, LANES), lambda b, h, i: (0, 0)),
            pl.BlockSpec((1, nc, tq, LANES), lambda b, h, i: (b, h, i, 0)),
            pl.BlockSpec((1, nc, seq, LANES), lambda b, h, i: (b, n_groups + h, 0, 0)),
            pl.BlockSpec((1, nc, seq, LANES), lambda b, h, i: (b, 2 * n_groups + h, 0, 0)),
            pl.BlockSpec((1, nc, tq, LANES), lambda b, h, i: (b, 3 * n_groups + h, i, 0)),
            pl.BlockSpec((1, 2 * LANES), lambda b, h, i: (0, 0)),
        ],
        out_specs=pl.BlockSpec((1, nc, tq, LANES), lambda b, h, i: (b, h, i, 0)),
        out_shape=jax.ShapeDtypeStruct((batch, 2 * n_heads, seq, LANES), BF16),
        scratch_shapes=[pltpu.VMEM((nc, tq, LANES), F32), pltpu.VMEM((nc, tq, LANES), F32),
                        pltpu.VMEM((nc, tq, 2 * LANES), F32)],
        compiler_params=pltpu.CompilerParams(
            dimension_semantics=("parallel", "parallel", "arbitrary"),
            vmem_limit_bytes=VMEM_LIMIT),
        name="df_attention",
    )(lamv, qk_gains, proj, proj, proj, proj, sub_g)


def _outproj_kernel(y_ref, w_ref, h_ref, g_ref, o_ref, *rest):
    xn_ref, wb_ref = rest if len(rest) == 2 else (None, rest[0])

    @pl.when((pl.program_id(0) == 0) & (pl.program_id(1) == 0))
    def _():
        wb_ref[...] = w_ref[0].astype(BF16)

    n_chunks = y_ref.shape[1]
    y = jnp.concatenate([y_ref[0, c] for c in range(n_chunks)], axis=1)
    h = h_ref[...] + jnp.dot(y, wb_ref[...], preferred_element_type=F32)
    o_ref[...] = h
    if xn_ref is not None:
        ms = jnp.mean(h * h, axis=-1, keepdims=True)
        xn_ref[...] = (h * lax.rsqrt(ms + EPS) * g_ref[...]).astype(BF16)


def _outproj(y, w_all, layer, h2d, g_next, *, emit_norm, tm=1024):
    batch, n_chunks, seq, _ = y.shape
    _, e, d = w_all.shape
    tm = min(tm, seq)
    spt = seq // tm
    row_spec = pl.BlockSpec((tm, d), lambda b, i: (b * spt + i, 0))
    return pl.pallas_call(
        _outproj_kernel,
        grid=(batch, spt),
        in_specs=[
            pl.BlockSpec((1, n_chunks, tm, LANES), lambda b, i: (b, 0, i, 0)),
            pl.BlockSpec((1, e, d), lambda b, i: (layer, 0, 0)),
            row_spec,
            pl.BlockSpec((1, d), lambda b, i: (0, 0)),
        ],
        out_specs=[row_spec, row_spec][:1 + emit_norm],
        out_shape=[jax.ShapeDtypeStruct(h2d.shape, F32),
                   jax.ShapeDtypeStruct(h2d.shape, BF16)][:1 + emit_norm],
        scratch_shapes=[pltpu.VMEM((e, d), BF16)],
        compiler_params=pltpu.CompilerParams(
            dimension_semantics=("arbitrary", "arbitrary"),
            vmem_limit_bytes=VMEM_LIMIT),
        name="outproj_residual",
    )(y, w_all, h2d, g_next)


def _rope_tables(seq):
    inv = 1.0 / (ROPE_THETA ** (jnp.arange(0, HEAD_DIM, 2, dtype=F32) / HEAD_DIM))
    ang = jnp.arange(seq, dtype=F32)[:, None] * inv[None, :]
    cos, sin = jnp.cos(ang), jnp.sin(ang)
    return jnp.concatenate([cos, cos], axis=1), jnp.concatenate([-sin, sin], axis=1)


def _suffix_matrix(n):
    j = jnp.arange(n)[:, None]
    s = jnp.arange(n)[None, :]
    return (j > s).astype(BF16)


def kernel(x, sb_norm, sb_w_in, sb_w_out, df_norm, df_w_in, df_w_out, df_q_norm, df_k_norm,
           df_lam_q1, df_lam_k1, df_lam_q2, df_lam_k2, df_sub_norm):
    batch, seq, d_model = x.shape
    e_width = sb_w_out.shape[1]
    depth = sb_norm.shape[0] + df_norm.shape[0]
    sb_heads = e_width // HEAD_DIM
    df_heads = e_width // (2 * HEAD_DIM)
    scale = 1.0 / math.sqrt(HEAD_DIM)

    cos2, sin2 = _rope_tables(seq)
    rmat = _suffix_matrix(SUFFIX_BLOCK)
    ones_e = jnp.ones((e_width,), F32)
    sb_colscale = jnp.concatenate([ones_e * (scale * LOG2E), ones_e, ones_e, ones_e])[None, :]

    def pre_norm_gain(i):
        norms = sb_norm if i % N_MIXERS == 0 else df_norm
        return norms[i // N_MIXERS][None, :]

    h = x.reshape(batch * seq, d_model)
    xin = h
    for i in range(depth):
        j = i // N_MIXERS
        last = i == depth - 1
        g_next = pre_norm_gain(i if last else i + 1)
        if i % N_MIXERS == 0:
            proj = _inproj(xin, pre_norm_gain(i), sb_w_in, j, sb_colscale,
                           cos2, sin2, batch=batch, seq=seq, n_rope_cols=0)
            y = _sb_attention(proj, rmat, n_heads=sb_heads)
            h, *rest = _outproj(y, sb_w_out, j, h, g_next, emit_norm=not last)
        else:
            lam_init = 0.8 - 0.6 * math.exp(-0.3 * i)
            colscale = jnp.concatenate([jnp.tile(df_q_norm[j], 2 * df_heads) * (scale * LOG2E),
                                        jnp.tile(df_k_norm[j], 2 * df_heads),
                                        ones_e, ones_e])[None, :]
            proj = _inproj(xin, pre_norm_gain(i), df_w_in, j, colscale,
                           cos2, sin2, batch=batch, seq=seq,
                           n_rope_cols=2 * e_width)
            lamv = jnp.stack([df_lam_q1[j], df_lam_k1[j], df_lam_q2[j], df_lam_k2[j]])
            qk_gains = jnp.stack([df_q_norm[j] * (scale * LOG2E), df_k_norm[j]])
            y = _df_attention(proj, lamv, qk_gains, df_sub_norm[j][None, :], n_heads=df_heads,
                              lam_init=lam_init)
            h, *rest = _outproj(y, df_w_out, j, h, g_next, emit_norm=not last)
        xin = rest[0] if rest else h
    return h.reshape(batch, seq, d_model)
```
